```python
import functools
import jax, jax.numpy as jnp
from jax import lax
import numpy as np

D_MODEL = 2048
BATCH = 4
SEQ = 2048
DEPTH = 4
DEC_BATCH = 32
DEC_SEQ = 8
PAST_LEN = 16384
PAGE_SIZE = 128

EXPAND = 2
E_WIDTH = EXPAND * D_MODEL
N_A = DEPTH // 2
N_B = DEPTH - N_A
DK_A = 128
H_A = D_MODEL // DK_A
DV_A = E_WIDTH // H_A
HD_B = 64
H_B = E_WIDTH // HD_B
KV_B = H_B // 8
G_B = H_B // KV_B
WINDOW = 128
N_META = 16
CHUNK = 16
EPS = 1e-6
NEG_BIG = -1e30
TINY = 1e-30

kernel_name = 'yoco_hgrn2_swa_sink_alibi_step'


def rmsnorm(x, g):
    x32 = x.astype(jnp.float32)
    y = x32 * lax.rsqrt(jnp.mean(x32 * x32, axis=-1, keepdims=True) + EPS)
    return (y * g.astype(jnp.float32)).astype(x.dtype)


def lower_bounds(lb_param):
    p = jax.nn.softmax(lb_param.astype(jnp.float32), axis=0)
    return jnp.cumsum(p, axis=0) - p[0]


def gla_chunk_scan(q, k, v, logf, s0):
    bsz, t_ = q.shape[:2]
    pad = (-t_) % CHUNK
    padw = ((0, 0), (0, pad), (0, 0), (0, 0))
    q, k, v, logf = [jnp.pad(a, padw) for a in (q, k, v, logf)]
    nc = (t_ + pad) // CHUNK

    def to_chunks(a):
        return jnp.moveaxis(a.reshape(bsz, nc, CHUNK, a.shape[2], a.shape[3]), 1, 0)

    causal = jnp.tril(jnp.ones((CHUNK, CHUNK), bool))[None, :, :, None, None]

    def step(S, inp):
        qc, kc, vc, gc = inp
        cum = jnp.cumsum(gc, axis=1)
        o_inter = jnp.einsum('bthd,bhde->bthe', qc * jnp.exp(cum), S)
        rel = cum[:, :, None] - cum[:, None, :]
        decay = jnp.where(causal, jnp.exp(jnp.where(causal, rel, 0.0)), 0.0)
        att = jnp.einsum('bthd,bshd,btshd->bhts', qc, kc, decay)
        o_intra = jnp.einsum('bhts,bshe->bthe', att, vc)
        last = cum[:, -1]
        S_new = jnp.exp(last)[..., None] * S + jnp.einsum(
            'bshd,bshe->bhde', kc * jnp.exp(last[:, None] - cum), vc)
        return S_new, o_inter + o_intra

    s_fin, o = lax.scan(step, s0, (to_chunks(q), to_chunks(k), to_chunks(v), to_chunks(logf)))
    o = jnp.moveaxis(o, 0, 1).reshape(bsz, nc * CHUNK, o.shape[-2], o.shape[-1])[:, :t_]
    return o, s_fin


def hgrn2_layer(x, s0, norm_g, w_in, lb, onorm_g, w_out):
    bsz, t_, _ = x.shape
    fd = H_A * DK_A
    z = rmsnorm(x, norm_g) @ w_in
    q, f, i, g = jnp.split(z, [fd, 2 * fd, 2 * fd + E_WIDTH], axis=-1)
    f = f.astype(jnp.float32)
    q = jax.nn.silu(q.astype(jnp.float32)) * (DK_A ** -0.5)
    base = jnp.log1p(-lb) + jax.nn.log_sigmoid(f)
    logf = jnp.where(lb > 0, jnp.logaddexp(jnp.log(jnp.maximum(lb, TINY)), base), base)
    k = (1.0 - lb) * jax.nn.sigmoid(-f)
    o, s_fin = gla_chunk_scan(q.reshape(bsz, t_, H_A, DK_A), k.reshape(bsz, t_, H_A, DK_A),
                              i.astype(jnp.float32).reshape(bsz, t_, H_A, DV_A),
                              logf.reshape(bsz, t_, H_A, DK_A), s0)
    o = rmsnorm(o, onorm_g).reshape(bsz, t_, E_WIDTH) * jax.nn.silu(g.astype(jnp.float32))
    return x + o.astype(x.dtype) @ w_out, s_fin


def alibi_slopes():
    h = jnp.arange(1, H_B + 1, dtype=jnp.float32)
    return (2.0 ** (-8.0 * h / H_B)).reshape(KV_B, G_B, 1, 1)


def sink_softmax_attention(q, k, v, dist, valid, sink):
    s = jnp.einsum('...ikgd,...jkd->...kgij', q.astype(jnp.float32), k.astype(jnp.float32)) * (HD_B ** -0.5)
    s = s - alibi_slopes() * dist.astype(jnp.float32)
    s = jnp.where(valid, s, NEG_BIG)
    m = jnp.maximum(jnp.max(s, axis=-1, keepdims=True), sink)
    p = jnp.where(valid, jnp.exp(s - m), 0.0)
    denom = jnp.sum(p, axis=-1, keepdims=True) + jnp.exp(sink - m)
    return jnp.einsum('...kgij,...jkd->...ikgd', p / denom, v.astype(jnp.float32))


def banded_attend(q, k, v, sink):
    bsz, t_ = q.shape[:2]
    nb = -(-t_ // WINDOW)
    pad = nb * WINDOW - t_
    qb = jnp.pad(q, ((0, 0), (0, pad), (0, 0), (0, 0), (0, 0))).reshape(bsz, nb, WINDOW, KV_B, G_B, HD_B)

    def band(a):
        a = jnp.pad(a, ((0, 0), (0, pad), (0, 0), (0, 0))).reshape(bsz, nb, WINDOW, KV_B, HD_B)
        prev = jnp.pad(a, ((0, 0), (1, 0), (0, 0), (0, 0), (0, 0)))[:, :-1]
        return jnp.concatenate([prev, a], axis=2)

    i = jnp.arange(WINDOW)[:, None]
    j = jnp.arange(2 * WINDOW)[None, :]
    dist = WINDOW + i - j
    valid = (dist >= 0) & (dist < WINDOW)
    valid = valid[None] & ((jnp.arange(nb)[:, None, None] > 0) | (j >= WINDOW)[None])
    o = sink_softmax_attention(qb, band(k), band(v), dist, valid[:, None, None], sink)
    return o.reshape(bsz, nb * WINDOW, KV_B, G_B, HD_B)[:, :t_]


def cached_attend(q, k, v, sink, past_k, past_v):
    wb = past_k.shape[1]
    t_ = q.shape[1]
    kf = jnp.concatenate([past_k.astype(k.dtype), k], axis=1)
    vf = jnp.concatenate([past_v.astype(v.dtype), v], axis=1)
    i = jnp.arange(t_)[:, None]
    j = jnp.arange(wb + t_)[None, :]
    dist = wb + i - j
    valid = (dist >= 0) & (dist < WINDOW)
    return sink_softmax_attention(q, kf, vf, dist, valid, sink)


def swa_layer(x, k_sh, v_sh, attend, norm_g, w_in, sink, w_out):
    bsz, t_, _ = x.shape
    z = rmsnorm(x, norm_g) @ w_in
    q, gate = jnp.split(z, 2, axis=-1)
    q = q.reshape(bsz, t_, KV_B, G_B, HD_B)
    o = attend(q, k_sh, v_sh, sink.astype(jnp.float32).reshape(KV_B, G_B, 1, 1))
    o = o.reshape(bsz, t_, E_WIDTH) * jax.nn.silu(gate.astype(jnp.float32))
    return x + o.astype(x.dtype) @ w_out


def trunk(x, s0, attend, norm_a, w_in_a, lb_a, onorm_a, w_out_a, norm_kv, w_kv,
          norm_b, w_in_b, sink_b, w_out_b, norm_f):
    bsz, t_, _ = x.shape
    lbs = lower_bounds(lb_a)
    states = []
    k_sh = v_sh = None
    for layer in range(DEPTH):
        if layer < N_A:
            x, s = hgrn2_layer(x, s0[layer], norm_a[layer], w_in_a[layer], lbs[layer],
                               onorm_a[layer], w_out_a[layer])
            states.append(s)
            if layer == N_A - 1:
                kv = rmsnorm(x, norm_kv) @ w_kv
                k_sh, v_sh = jnp.split(kv, 2, axis=-1)
                k_sh = k_sh.reshape(bsz, t_, KV_B, HD_B)
                v_sh = v_sh.reshape(bsz, t_, KV_B, HD_B)
        else:
            lb_i = layer - N_A
            x = swa_layer(x, k_sh, v_sh, attend, norm_b[lb_i], w_in_b[lb_i], sink_b[lb_i], w_out_b[lb_i])
    return rmsnorm(x, norm_f), jnp.stack(states), k_sh, v_sh


def setup_inputs(seed: int = 0) -> dict:
    key = jax.random.key(seed)
    ks = jax.random.split(key, 20)
    f32 = jnp.float32
    wb = min(WINDOW, PAST_LEN)
    za = 2 * H_A * DK_A + 2 * E_WIDTH

    def nrm(k, shape, s):
        return s * jax.random.normal(k, shape, f32)

    return {
        'x_prompt': nrm(ks[0], (BATCH, SEQ, D_MODEL), 1.0),
        'x_sample': nrm(ks[1], (DEC_BATCH, DEC_SEQ, D_MODEL), 1.0),
        'state_hgrn': nrm(ks[2], (N_A, DEC_BATCH, H_A, DK_A, DV_A), 0.3),
        'cache_k': nrm(ks[3], (DEC_BATCH, wb, KV_B, HD_B), 1.0),
        'cache_v': nrm(ks[4], (DEC_BATCH, wb, KV_B, HD_B), 1.0),
        'meta_tokens': nrm(ks[5], (N_META, D_MODEL), 1.0),
        'norm_a': 1.0 + nrm(ks[6], (N_A, D_MODEL), 0.02),
        'w_in_a': nrm(ks[7], (N_A, D_MODEL, za), D_MODEL ** -0.5),
        'lb_a': nrm(ks[8], (N_A, H_A * DK_A), 0.5),
        'onorm_a': 1.0 + nrm(ks[9], (N_A, DV_A), 0.02),
        'w_out_a': nrm(ks[10], (N_A, E_WIDTH, D_MODEL), E_WIDTH ** -0.5),
        'norm_kv': 1.0 + nrm(ks[11], (D_MODEL,), 0.02),
        'w_kv': nrm(ks[12], (D_MODEL, 2 * KV_B * HD_B), D_MODEL ** -0.5),
        'norm_b': 1.0 + nrm(ks[13], (N_B, D_MODEL), 0.02),
        'w_in_b': nrm(ks[14], (N_B, D_MODEL, 2 * E_WIDTH), D_MODEL ** -0.5),
        'sink_b': nrm(ks[15], (N_B, H_B), 0.5),
        'w_out_b': nrm(ks[16], (N_B, E_WIDTH, D_MODEL), E_WIDTH ** -0.5),
        'norm_f': 1.0 + nrm(ks[17], (D_MODEL,), 0.02),
    }


def reference(x_prompt, x_sample, state_hgrn, cache_k, cache_v, meta_tokens, norm_a, w_in_a, lb_a,
              onorm_a, w_out_a, norm_kv, w_kv, norm_b, w_in_b, sink_b, w_out_b, norm_f):
    weights = (norm_a, w_in_a, lb_a, onorm_a, w_out_a, norm_kv, w_kv, norm_b, w_in_b, sink_b, w_out_b, norm_f)
    bsz = x_prompt.shape[0]
    wb = cache_k.shape[1]
    meta = jnp.broadcast_to(meta_tokens.astype(x_prompt.dtype)[None], (bsz, N_META, D_MODEL))
    xp = jnp.concatenate([meta, x_prompt], axis=1)
    s0p = jnp.zeros((N_A, bsz, H_A, DK_A, DV_A), jnp.float32)
    yp, sp, kp, vp = trunk(xp, s0p, banded_attend, *weights)
    attend_s = functools.partial(cached_attend, past_k=cache_k, past_v=cache_v)
    ys, ss, k_new, v_new = trunk(x_sample, state_hgrn.astype(jnp.float32), attend_s, *weights)
    y_prompt = yp[:, N_META:]
    state_hgrn_prompt = sp.astype(state_hgrn.dtype)
    cache_k_prompt = kp[:, -wb:].astype(cache_k.dtype)
    cache_v_prompt = vp[:, -wb:].astype(cache_v.dtype)
    state_hgrn_sample = ss.astype(state_hgrn.dtype)
    cache_k_sample = jnp.concatenate([cache_k, k_new.astype(cache_k.dtype)], axis=1)[:, -wb:]
    cache_v_sample = jnp.concatenate([cache_v, v_new.astype(cache_v.dtype)], axis=1)[:, -wb:]
    return (y_prompt, ys, state_hgrn_prompt, cache_k_prompt, cache_v_prompt,
            state_hgrn_sample, cache_k_sample, cache_v_sample)
```

```python
import functools

import jax
import jax.numpy as jnp
from jax import lax
from jax.experimental import pallas as pl
from jax.experimental.pallas import tpu as pltpu

D_MODEL = 2048
E_WIDTH = 2 * D_MODEL
N_A = 2
N_B = 2
DK_A = 128
H_A = D_MODEL // DK_A
DV_A = E_WIDTH // H_A
HD_B = 64
H_B = E_WIDTH // HD_B
KV_B = H_B // 8
G_B = H_B // KV_B
WINDOW = 128
N_META = 16
EPS = 1e-6
NEG_BIG = -1e30
TINY = 1e-30

LANES = 128
ROW_BLOCK = 128
PAD = ROW_BLOCK - N_META
FACTOR_SAFE = 60.0
LN2 = 0.6931471805599453
BF16_SUBLANES = 16
VMEM_LIMIT = 48 * 1024 * 1024

F32 = jnp.float32
BF16 = jnp.bfloat16


def _sigmoid(x):
    return 1.0 / (1.0 + jnp.exp(-x))


def _params(*sem):
    return pltpu.CompilerParams(dimension_semantics=sem, vmem_limit_bytes=VMEM_LIMIT)


def _norm_matmul_kernel(x_ref, g_ref, w_ref, o_ref, xn_ref):
    @pl.when(pl.program_id(1) == 0)
    def _():
        x = x_ref[...]
        ms = jnp.mean(x * x, axis=-1, keepdims=True)
        xn_ref[...] = (x * lax.rsqrt(ms + EPS) * g_ref[...]).astype(BF16)

    o_ref[...] = jnp.dot(xn_ref[...], w_ref[...], preferred_element_type=F32)


def norm_matmul(x, g, w, tm, tn):
    n, d = x.shape
    m = w.shape[1]
    return pl.pallas_call(
        _norm_matmul_kernel,
        grid=(n // tm, m // tn),
        in_specs=[
            pl.BlockSpec((tm, d), lambda i, j: (i, 0)),
            pl.BlockSpec((1, d), lambda i, j: (0, 0)),
            pl.BlockSpec((d, tn), lambda i, j: (0, j)),
        ],
        out_specs=pl.BlockSpec((tm, tn), lambda i, j: (i, j)),
        out_shape=jax.ShapeDtypeStruct((n, m), F32),
        scratch_shapes=[pltpu.VMEM((tm, d), BF16)],
        compiler_params=_params("arbitrary", "arbitrary"),
        name="norm_matmul",
    )(x, g.reshape(1, d), w)


def _kv_proj_kernel(x_ref, g_ref, w_ref, o_ref):
    x = x_ref[...]
    ms = jnp.mean(x * x, axis=-1, keepdims=True)
    xn = (x * lax.rsqrt(ms + EPS) * g_ref[...]).astype(BF16)
    kv = jnp.dot(xn, w_ref[...], preferred_element_type=F32)
    for j in range(2 * KV_B):
        o_ref[j] = kv[:, j * HD_B:(j + 1) * HD_B]


def kv_proj(x, g, w, tm):
    n, d = x.shape
    return pl.pallas_call(
        _kv_proj_kernel,
        grid=(n // tm,),
        in_specs=[
            pl.BlockSpec((tm, d), lambda i: (i, 0)),
            pl.BlockSpec((1, d), lambda i: (0, 0)),
            pl.BlockSpec((d, 2 * KV_B * HD_B), lambda i: (0, 0)),
        ],
        out_specs=pl.BlockSpec((2 * KV_B, tm, HD_B), lambda i: (0, i, 0)),
        out_shape=jax.ShapeDtypeStruct((2 * KV_B, n, HD_B), F32),
        compiler_params=_params("arbitrary"),
        name="kv_proj",
    )(x, g.reshape(1, d), w)


def _matmul_res_kernel(y_ref, w_ref, x_ref, o_ref):
    o_ref[...] = x_ref[...] + jnp.dot(y_ref[...].astype(BF16), w_ref[...], preferred_element_type=F32)


def _act_dtype(rows_per_block):
    return BF16 if rows_per_block % BF16_SUBLANES == 0 else F32


def matmul_residual(y, w, x, tm, tn):
    n, e = y.shape
    d = w.shape[1]
    return pl.pallas_call(
        _matmul_res_kernel,
        grid=(n // tm, d // tn),
        in_specs=[
            pl.BlockSpec((tm, e), lambda i, j: (i, 0)),
            pl.BlockSpec((e, tn), lambda i, j: (0, j)),
            pl.BlockSpec((tm, tn), lambda i, j: (i, j)),
        ],
        out_specs=pl.BlockSpec((tm, tn), lambda i, j: (i, j)),
        out_shape=jax.ShapeDtypeStruct((n, d), F32),
        compiler_params=_params("arbitrary", "arbitrary"),
        name="matmul_residual",
    )(y, w, x)


def _rmsnorm_kernel(x_ref, g_ref, o_ref):
    x = x_ref[...]
    ms = jnp.mean(x * x, axis=-1, keepdims=True)
    o_ref[...] = x * lax.rsqrt(ms + EPS) * g_ref[...]


def final_norm(x, g, tm, n_out_blocks, in_block_of):
    d = x.shape[1]
    return pl.pallas_call(
        _rmsnorm_kernel,
        grid=(n_out_blocks,),
        in_specs=[
            pl.BlockSpec((tm, d), lambda t: (in_block_of(t), 0)),
            pl.BlockSpec((1, d), lambda t: (0, 0)),
        ],
        out_specs=pl.BlockSpec((tm, d), lambda t: (t, 0)),
        out_shape=jax.ShapeDtypeStruct((n_out_blocks * tm, d), F32),
        compiler_params=_params("arbitrary"),
        name="final_norm",
    )(x, g.reshape(1, d))


def _hgrn_kernel(*refs, layer, chunk, hb, zero_init):
    if zero_init:
        zq_ref, zf_ref, zi_ref, zg_ref, lb_ref, on_ref, y_ref, sfin_ref, s_ref, oi_ref = refs
        s0_ref = None
    else:
        zq_ref, zf_ref, zi_ref, zg_ref, lb_ref, on_ref, s0_ref, y_ref, sfin_ref, s_ref, oi_ref = refs
    c = chunk
    t = pl.program_id(2)

    @pl.when(t == 0)
    def _():
        if zero_init:
            s_ref[...] = jnp.zeros_like(s_ref)
        else:
            s_ref[...] = s0_ref[0]

    lbp = lb_ref[...]
    e = jnp.exp(lbp - jnp.max(lbp, axis=0, keepdims=True))
    p = e / jnp.sum(e, axis=0, keepdims=True)
    acc = p[0:1]
    for j in range(1, layer + 1):
        acc = acc + p[j:j + 1]
    lb = acc - p[0:1]

    qz = zq_ref[...]
    fz = zf_ref[...]
    q = qz * _sigmoid(qz) * (DK_A ** -0.5)
    ls = jnp.minimum(fz, 0.0) - jnp.log1p(jnp.exp(-jnp.abs(fz)))
    base = jnp.log1p(-lb) + ls
    loglb = jnp.log(jnp.maximum(lb, TINY))
    lae = jnp.maximum(loglb, base) + jnp.log1p(jnp.exp(-jnp.abs(loglb - base)))
    logf = jnp.where(lb > 0, lae, base)
    k = (1.0 - lb) * _sigmoid(-fz)

    ri = lax.broadcasted_iota(jnp.int32, (c, c), 0)
    ci = lax.broadcasted_iota(jnp.int32, (c, c), 1)
    tril = (ci <= ri).astype(BF16)
    h1 = logf.astype(BF16)
    r1 = logf - h1.astype(F32)
    h2 = r1.astype(BF16)
    h3 = (r1 - h2.astype(F32)).astype(BF16)
    cum = (jnp.dot(tril, h1, preferred_element_type=F32)
           + jnp.dot(tril, h2, preferred_element_type=F32)
           + jnp.dot(tril, h3, preferred_element_type=F32))

    last = cum[c - 1:c, :]
    mid = cum[c // 2 - 1:c // 2, :]
    qa = q * jnp.exp(cum)
    kd = k * jnp.exp(last - cum)
    el = jnp.exp(last)
    dev = jnp.max(jnp.abs(cum - mid))
    factor_ok = dev <= FACTOR_SAFE

    row_c = lax.broadcasted_iota(jnp.int32, (c, LANES), 0)
    col_c = lax.broadcasted_iota(jnp.int32, (c, LANES), 1)
    causal = col_c <= row_c
    zpad_k = jnp.zeros((LANES - c, DK_A), F32)
    zpad_v = jnp.zeros((LANES - c, DV_A), F32)

    def v_padded(h):
        v = zi_ref[:, h * DV_A:(h + 1) * DV_A]
        return jnp.concatenate([v, zpad_v], axis=0).astype(BF16)

    @pl.when(factor_ok)
    def _():
        for h in range(hb):
            hs = slice(h * DK_A, (h + 1) * DK_A)
            d_h = cum[:, hs] - mid[:, hs]
            qm = (q[:, hs] * jnp.exp(d_h)).astype(BF16)
            km = k[:, hs] * jnp.exp(-d_h)
            km = jnp.concatenate([km, zpad_k], axis=0).astype(BF16)
            att = lax.dot_general(qm, km, (((1,), (1,)), ((), ())), preferred_element_type=F32)
            att = jnp.where(causal, att, 0.0).astype(BF16)
            oi_ref[:, h * DV_A:(h + 1) * DV_A] = jnp.dot(att, v_padded(h), preferred_element_type=F32)

    @pl.when(jnp.logical_not(factor_ok))
    def _():
        row_v = lax.broadcasted_iota(jnp.int32, (c, 1), 0)

        def row_of(a, s):
            return jnp.sum(jnp.where(row_v == s, a, 0.0), axis=0, keepdims=True)

        for h in range(hb):
            hs = slice(h * DK_A, (h + 1) * DK_A)
            vs_ = slice(h * DV_A, (h + 1) * DV_A)
            q_h = q[:, hs]
            k_h = k[:, hs]
            cum_h = cum[:, hs]
            v_h = zi_ref[:, vs_]

            def body(s, acc_o, q_h=q_h, k_h=k_h, cum_h=cum_h, v_h=v_h):
                dec = jnp.exp(jnp.minimum(cum_h - row_of(cum_h, s), 0.0))
                w = jnp.sum(q_h * row_of(k_h, s) * dec, axis=-1, keepdims=True)
                w = jnp.where(row_v >= s, w, 0.0)
                return acc_o + w * row_of(v_h, s)

            oi_ref[:, vs_] = lax.fori_loop(0, c, body, jnp.zeros((c, DV_A), F32))

    on_g = on_ref[...]
    el8 = jnp.broadcast_to(el, (8, el.shape[1]))
    zpad_x = jnp.zeros((LANES - c - 8, DK_A), F32)
    for h in range(hb):
        hs = slice(h * DK_A, (h + 1) * DK_A)
        vs_ = slice(h * DV_A, (h + 1) * DV_A)
        s_prev = s_ref[h]
        o = jnp.dot(qa[:, hs].astype(BF16), s_prev.astype(BF16), preferred_element_type=F32)
        o = o + oi_ref[:, vs_]
        x_t = jnp.concatenate([kd[:, hs], el8[:, hs], zpad_x], axis=0).T
        ds = jnp.dot(x_t.astype(BF16), v_padded(h), preferred_element_type=F32)
        s_ref[h] = x_t[:, c:c + 1] * s_prev + ds
        ms = jnp.mean(o * o, axis=-1, keepdims=True)
        o = o * lax.rsqrt(ms + EPS) * on_g
        g = zg_ref[:, vs_]
        y_ref[:, vs_] = (o * (g * _sigmoid(g))).astype(y_ref.dtype)

    @pl.when(t == pl.num_programs(2) - 1)
    def _():
        sfin_ref[0] = s_ref[...]


def hgrn_scan(z, lb_a, onorm_g, s0, *, layer, n_batch, rows_per_batch, chunk, hb):
    n = z.shape[0]
    nt = rows_per_batch // chunk
    ng = H_A // hb
    wk, wv = hb * DK_A, hb * DV_A
    f_off = (H_A * DK_A) // wk
    i_off = (2 * H_A * DK_A) // wv
    g_off = (2 * H_A * DK_A + E_WIDTH) // wv
    row = lambda b, g, t: b * nt + t
    in_specs = [
        pl.BlockSpec((chunk, wk), lambda b, g, t: (row(b, g, t), g)),
        pl.BlockSpec((chunk, wk), lambda b, g, t: (row(b, g, t), f_off + g)),
        pl.BlockSpec((chunk, wv), lambda b, g, t: (row(b, g, t), i_off + g)),
        pl.BlockSpec((chunk, wv), lambda b, g, t: (row(b, g, t), g_off + g)),
        pl.BlockSpec((N_A, wk), lambda b, g, t: (0, g)),
        pl.BlockSpec((1, DV_A), lambda b, g, t: (0, 0)),
    ]
    args = [z, z, z, z, lb_a, onorm_g.reshape(1, DV_A)]
    state_spec = pl.BlockSpec((1, hb, DK_A, DV_A), lambda b, g, t: (b, g, 0, 0))
    if s0 is not None:
        in_specs.append(state_spec)
        args.append(s0)
    kern = functools.partial(_hgrn_kernel, layer=layer, chunk=chunk, hb=hb, zero_init=s0 is None)
    return pl.pallas_call(
        kern,
        grid=(n_batch, ng, nt),
        in_specs=in_specs,
        out_specs=[
            pl.BlockSpec((chunk, wv), lambda b, g, t: (row(b, g, t), g)),
            state_spec,
        ],
        out_shape=[
            jax.ShapeDtypeStruct((n, E_WIDTH), _act_dtype(chunk)),
            jax.ShapeDtypeStruct((n_batch, H_A, DK_A, DV_A), F32),
        ],
        scratch_shapes=[
            pltpu.VMEM((hb, DK_A, DV_A), F32),
            pltpu.VMEM((chunk, wv), F32),
        ],
        compiler_params=_params("arbitrary", "arbitrary", "arbitrary"),
        name=f"hgrn_scan_c{chunk}",
    )(*args)


def _swa_prompt_kernel(sink_ref, q_ref, g_ref, kc_ref, kp_ref, vc_ref, vp_ref, y_ref, s_sc, p_sc):
    blk = pl.program_id(1)
    kh = pl.program_id(2)
    w = WINDOW
    row = lax.broadcasted_iota(jnp.int32, (w, 2 * w), 0)
    col = lax.broadcasted_iota(jnp.int32, (w, 2 * w), 1)
    dist = w + row - col
    key_pos = (blk - 1) * w + col
    valid = (dist >= 0) & (dist < w) & (key_pos >= PAD)
    distf = dist.astype(F32)

    kcat = jnp.concatenate([kp_ref[0], kc_ref[0]], axis=0).astype(BF16)
    vcat = jnp.concatenate([vp_ref[0], vc_ref[0]], axis=0).astype(BF16)
    qs = jnp.concatenate([q_ref[:, g * HD_B:(g + 1) * HD_B] for g in range(G_B)], axis=0)
    s_sc[...] = lax.dot_general(qs.astype(BF16), kcat, (((1,), (1,)), ((), ())),
                                preferred_element_type=F32)
    dens = []
    for g in range(G_B):
        head = kh * G_B + g
        slope = jnp.exp(-LN2 * 8.0 * (head + 1).astype(F32) / H_B)
        sink = sink_ref[head]
        s = s_sc[g * w:(g + 1) * w, :] * (HD_B ** -0.5) - slope * distf
        s = jnp.where(valid, s, NEG_BIG)
        m = jnp.maximum(jnp.max(s, axis=-1, keepdims=True), sink)
        p = jnp.where(valid, jnp.exp(s - m), 0.0)
        dens.append(jnp.sum(p, axis=-1, keepdims=True) + jnp.exp(sink - m))
        p_sc[g * w:(g + 1) * w, :] = p.astype(BF16)
    o_all = jnp.dot(p_sc[...], vcat, preferred_element_type=F32)
    for g in range(G_B):
        o = o_all[g * w:(g + 1) * w, :] / dens[g]
        gate = g_ref[:, g * HD_B:(g + 1) * HD_B]
        y_ref[:, g * HD_B:(g + 1) * HD_B] = (o * (gate * _sigmoid(gate))).astype(BF16)


def swa_prompt(zb, kv_hm, sink, n_batch, blocks_per_batch):
    n = zb.shape[0]
    w = WINDOW
    gw = G_B * HD_B
    nb = blocks_per_batch
    cur = lambda b, i, kh: b * nb + i
    prev = lambda b, i, kh: b * nb + jnp.maximum(i - 1, 0)
    return pl.pallas_call(
        _swa_prompt_kernel,
        grid=(n_batch, nb, KV_B),
        in_specs=[
            pl.BlockSpec(memory_space=pltpu.SMEM),
            pl.BlockSpec((w, gw), lambda b, i, kh: (cur(b, i, kh), kh)),
            pl.BlockSpec((w, gw), lambda b, i, kh: (cur(b, i, kh), KV_B + kh)),
            pl.BlockSpec((1, w, HD_B), lambda b, i, kh: (kh, cur(b, i, kh), 0)),
            pl.BlockSpec((1, w, HD_B), lambda b, i, kh: (kh, prev(b, i, kh), 0)),
            pl.BlockSpec((1, w, HD_B), lambda b, i, kh: (KV_B + kh, cur(b, i, kh), 0)),
            pl.BlockSpec((1, w, HD_B), lambda b, i, kh: (KV_B + kh, prev(b, i, kh), 0)),
        ],
        out_specs=pl.BlockSpec((w, gw), lambda b, i, kh: (cur(b, i, kh), kh)),
        out_shape=jax.ShapeDtypeStruct((n, E_WIDTH), BF16),
        scratch_shapes=[pltpu.VMEM((G_B * w, 2 * w), F32), pltpu.VMEM((G_B * w, 2 * w), BF16)],
        compiler_params=_params("arbitrary", "arbitrary", "arbitrary"),
        name="swa_prompt",
    )(sink, zb, zb, kv_hm, kv_hm, kv_hm, kv_hm)


def _swa_sample_kernel(sink_ref, q_ref, g_ref, kn_ref, vn_ref, ck_ref, cv_ref, y_ref, *, t_new):
    wb = ck_ref.shape[1]
    nk = wb + t_new
    nr = G_B * t_new
    row = lax.broadcasted_iota(jnp.int32, (nr, nk), 0)
    col = lax.broadcasted_iota(jnp.int32, (nr, nk), 1)
    tok = row % t_new
    grp = row // t_new
    dist = wb + tok - col
    valid = (dist >= 0) & (dist < WINDOW)
    distf = dist.astype(F32)
    grp1 = lax.broadcasted_iota(jnp.int32, (nr, 1), 0) // t_new
    for kh in range(KV_B):
        ks = slice(kh * HD_B, (kh + 1) * HD_B)
        kf = jnp.concatenate([ck_ref[0, :, ks], kn_ref[kh]], axis=0).astype(BF16)
        vf = jnp.concatenate([cv_ref[0, :, ks], vn_ref[kh]], axis=0).astype(BF16)
        qs = jnp.concatenate(
            [q_ref[:, (kh * G_B + g) * HD_B:(kh * G_B + g + 1) * HD_B] for g in range(G_B)], axis=0)
        s = lax.dot_general(qs.astype(BF16), kf, (((1,), (1,)), ((), ())), preferred_element_type=F32)
        slope = jnp.exp(-LN2 * 8.0 * (kh * G_B + grp + 1).astype(F32) / H_B)
        sink = jnp.zeros((nr, 1), F32)
        for g in range(G_B):
            sink = jnp.where(grp1 == g, sink_ref[kh * G_B + g], sink)
        s = s * (HD_B ** -0.5) - slope * distf
        s = jnp.where(valid, s, NEG_BIG)
        m = jnp.maximum(jnp.max(s, axis=-1, keepdims=True), sink)
        p = jnp.where(valid, jnp.exp(s - m), 0.0)
        den = jnp.sum(p, axis=-1, keepdims=True) + jnp.exp(sink - m)
        o = jnp.dot(p.astype(BF16), vf, preferred_element_type=F32) / den
        for g in range(G_B):
            hsl = slice((kh * G_B + g) * HD_B, (kh * G_B + g + 1) * HD_B)
            gate = g_ref[:, hsl]
            y_ref[:, hsl] = (o[g * t_new:(g + 1) * t_new, :] * (gate * _sigmoid(gate))).astype(y_ref.dtype)


def swa_sample(zb, kv_hm, sink, cache_k, cache_v, n_batch, t_new):
    n = zb.shape[0]
    wb = cache_k.shape[1]
    kern = functools.partial(_swa_sample_kernel, t_new=t_new)
    return pl.pallas_call(
        kern,
        grid=(n_batch,),
        in_specs=[
            pl.BlockSpec(memory_space=pltpu.SMEM),
            pl.BlockSpec((t_new, E_WIDTH), lambda b: (b, 0)),
            pl.BlockSpec((t_new, E_WIDTH), lambda b: (b, 1)),
            pl.BlockSpec((KV_B, t_new, HD_B), lambda b: (0, b, 0)),
            pl.BlockSpec((KV_B, t_new, HD_B), lambda b: (1, b, 0)),
            pl.BlockSpec((1, wb, KV_B * HD_B), lambda b: (b, 0, 0)),
            pl.BlockSpec((1, wb, KV_B * HD_B), lambda b: (b, 0, 0)),
        ],
        out_specs=pl.BlockSpec((t_new, E_WIDTH), lambda b: (b, 0)),
        out_shape=jax.ShapeDtypeStruct((n, E_WIDTH), _act_dtype(t_new)),
        compiler_params=_params("arbitrary"),
        name="swa_sample",
    )(sink, zb, zb, kv_hm, kv_hm, cache_k, cache_v)


def _trunk(x, s0, attend, weights, *, n_batch, rows_per_batch, chunk, hb, tm):
    (norm_a, w_in_a, lb_a, onorm_a, w_out_a, norm_kv, w_kv, norm_b, w_in_b, sink_b, w_out_b) = weights
    states = []
    for layer in range(N_A):
        z = norm_matmul(x, norm_a[layer], w_in_a[layer], tm, 1024)
        y, s_fin = hgrn_scan(z, lb_a, onorm_a[layer], None if s0 is None else s0[layer], layer=layer,
                             n_batch=n_batch, rows_per_batch=rows_per_batch, chunk=chunk, hb=hb)
        states.append(s_fin)
        x = matmul_residual(y, w_out_a[layer], x, tm, 512)
    kv_hm = kv_proj(x, norm_kv, w_kv, tm)
    for layer in range(N_B):
        zb = norm_matmul(x, norm_b[layer], w_in_b[layer], tm, 1024)
        y = attend(zb, kv_hm, sink_b[layer])
        x = matmul_residual(y, w_out_b[layer], x, tm, 512)
    return x, jnp.stack(states), kv_hm


def kernel(x_prompt, x_sample, state_hgrn, cache_k, cache_v, meta_tokens, norm_a, w_in_a, lb_a, onorm_a,
           w_out_a, norm_kv, w_kv, norm_b, w_in_b, sink_b, w_out_b, norm_f):
    bsz, seq, d = x_prompt.shape
    dec_b, dec_t, _ = x_sample.shape
    wb = cache_k.shape[1]
    weights = (norm_a, w_in_a.astype(BF16), lb_a, onorm_a, w_out_a.astype(BF16), norm_kv, w_kv.astype(BF16),
               norm_b, w_in_b.astype(BF16), sink_b, w_out_b.astype(BF16))

    rows_p = PAD + N_META + seq
    nb_p = rows_p // ROW_BLOCK
    xp = jnp.concatenate([
        jnp.zeros((bsz, PAD, d), x_prompt.dtype),
        jnp.broadcast_to(meta_tokens.astype(x_prompt.dtype)[None], (bsz, N_META, d)),
        x_prompt], axis=1).reshape(bsz * rows_p, d)
    attend_p = functools.partial(swa_prompt, n_batch=bsz, blocks_per_batch=nb_p)
    xp, sp, kvp = _trunk(xp, None, attend_p, weights, n_batch=bsz, rows_per_batch=rows_p, chunk=64, hb=8,
                         tm=512)
    blocks_out = seq // ROW_BLOCK
    y_prompt = final_norm(xp, norm_f, ROW_BLOCK, bsz * blocks_out,
                          lambda t: (t // blocks_out) * nb_p + 1 + t % blocks_out).reshape(bsz, seq, d)
    kvp = kvp.reshape(2, KV_B, bsz, rows_p, HD_B)[:, :, :, rows_p - wb:, :]
    cache_k_prompt = jnp.transpose(kvp[0], (1, 2, 0, 3)).astype(cache_k.dtype)
    cache_v_prompt = jnp.transpose(kvp[1], (1, 2, 0, 3)).astype(cache_v.dtype)

    xs = x_sample.reshape(dec_b * dec_t, d)
    attend_s = functools.partial(swa_sample, cache_k=cache_k.reshape(dec_b, wb, KV_B * HD_B),
                                 cache_v=cache_v.reshape(dec_b, wb, KV_B * HD_B), n_batch=dec_b, t_new=dec_t)
    xs, ss, kvs = _trunk(xs, state_hgrn.astype(F32), attend_s, weights, n_batch=dec_b, rows_per_batch=dec_t,
                         chunk=dec_t, hb=8, tm=dec_b * dec_t)
    y_sample = final_norm(xs, norm_f, dec_b * dec_t, 1, lambda t: t).reshape(dec_b, dec_t, d)
    kvs = kvs.reshape(2, KV_B, dec_b, dec_t, HD_B)
    k_new = jnp.transpose(kvs[0], (1, 2, 0, 3)).astype(cache_k.dtype)
    v_new = jnp.transpose(kvs[1], (1, 2, 0, 3)).astype(cache_v.dtype)
    cache_k_sample = jnp.concatenate([cache_k, k_new], axis=1)[:, -wb:]
    cache_v_sample = jnp.concatenate([cache_v, v_new], axis=1)[:, -wb:]

    return (y_prompt, y_sample, sp.astype(state_hgrn.dtype), cache_k_prompt, cache_v_prompt,
            ss.astype(state_hgrn.dtype), cache_k_sample, cache_v_sample)
```

```python
import functools

import jax
import jax.numpy as jnp
from jax import lax
from jax.experimental import pallas as pl
from jax.experimental.pallas import tpu as pltpu

D_MODEL = 2048
E_WIDTH = 2 * D_MODEL
N_A = 2
N_B = 2
DK_A = 128
H_A = D_MODEL // DK_A
DV_A = E_WIDTH // H_A
HD_B = 64
H_B = E_WIDTH // HD_B
KV_B = H_B // 8
G_B = H_B // KV_B
WINDOW = 128
N_META = 16
EPS = 1e-6
NEG_BIG = -1e30
TINY = 1e-30

LANES = 128
BF16_SUBLANES = 16
ROW_BLOCK = 128
PAD = ROW_BLOCK - N_META
FACTOR_SAFE = 60.0
LN2 = 0.6931471805599453
VMEM_LIMIT = 56 * 1024 * 1024
KH_STEP = 2

F32 = jnp.float32
BF16 = jnp.bfloat16


def _sigmoid(x):
    return 1.0 / (1.0 + jnp.exp(-x))


def _params(*sem):
    return pltpu.CompilerParams(dimension_semantics=sem, vmem_limit_bytes=VMEM_LIMIT)


def _row_tile(n, cap):
    for t in range(min(cap, n) // 8 * 8, 0, -8):
        if n % t == 0:
            return t
    raise ValueError(f"no row tile for {n}")


def _act_dtype(rows_per_block):
    return BF16 if rows_per_block % BF16_SUBLANES == 0 else F32


def _norm_matmul_kernel(x_ref, g_ref, w_ref, o_ref, xn_ref):
    @pl.when(pl.program_id(1) == 0)
    def _():
        x = x_ref[...]
        ms = jnp.mean(x * x, axis=-1, keepdims=True)
        xn_ref[...] = (x * lax.rsqrt(ms + EPS) * g_ref[...]).astype(BF16)

    o_ref[...] = jnp.dot(xn_ref[...], w_ref[...], preferred_element_type=F32)


def norm_matmul(x, g_all, w_all, layer, tn=1024):
    n, d = x.shape
    nl, _, m = w_all.shape
    tm = _row_tile(n, 1088)
    return pl.pallas_call(
        _norm_matmul_kernel,
        grid=(n // tm, m // tn),
        in_specs=[
            pl.BlockSpec((tm, d), lambda i, j: (i, 0)),
            pl.BlockSpec((None, 1, d), lambda i, j: (layer, 0, 0)),
            pl.BlockSpec((None, d, tn), lambda i, j: (layer, 0, j)),
        ],
        out_specs=pl.BlockSpec((tm, tn), lambda i, j: (i, j)),
        out_shape=jax.ShapeDtypeStruct((n, m), F32),
        scratch_shapes=[pltpu.VMEM((tm, d), BF16)],
        compiler_params=_params("arbitrary", "arbitrary"),
        name="norm_matmul",
    )(x, g_all.reshape(nl, 1, d), w_all)


def _kv_proj_kernel(x_ref, g_ref, w_ref, o_ref):
    x = x_ref[...]
    ms = jnp.mean(x * x, axis=-1, keepdims=True)
    xn = (x * lax.rsqrt(ms + EPS) * g_ref[...]).astype(BF16)
    kv = jnp.dot(xn, w_ref[...], preferred_element_type=F32)
    for j in range(2 * KV_B):
        o_ref[j] = kv[:, j * HD_B:(j + 1) * HD_B]


def kv_proj(x, g, w):
    n, d = x.shape
    tm = _row_tile(n, 544)
    return pl.pallas_call(
        _kv_proj_kernel,
        grid=(n // tm,),
        in_specs=[
            pl.BlockSpec((tm, d), lambda i: (i, 0)),
            pl.BlockSpec((1, d), lambda i: (0, 0)),
            pl.BlockSpec((d, 2 * KV_B * HD_B), lambda i: (0, 0)),
        ],
        out_specs=pl.BlockSpec((2 * KV_B, tm, HD_B), lambda i: (0, i, 0)),
        out_shape=jax.ShapeDtypeStruct((2 * KV_B, n, HD_B), F32),
        compiler_params=_params("arbitrary"),
        name="kv_proj",
    )(x, g.reshape(1, d), w)


def _matmul_res_kernel(y_ref, w_ref, x_ref, o_ref):
    o_ref[...] = x_ref[...] + jnp.dot(y_ref[...].astype(BF16), w_ref[...], preferred_element_type=F32)


def matmul_residual(y, w_all, layer, x):
    n, e = y.shape
    d = w_all.shape[2]
    tm = _row_tile(n, 544)
    return pl.pallas_call(
        _matmul_res_kernel,
        grid=(n // tm,),
        in_specs=[
            pl.BlockSpec((tm, e), lambda i: (i, 0)),
            pl.BlockSpec((None, e, d), lambda i: (layer, 0, 0), pipeline_mode=pl.Buffered(1)),
            pl.BlockSpec((tm, d), lambda i: (i, 0)),
        ],
        out_specs=pl.BlockSpec((tm, d), lambda i: (i, 0)),
        out_shape=jax.ShapeDtypeStruct((n, d), F32),
        compiler_params=_params("arbitrary"),
        name="matmul_residual",
    )(y, w_all, x)


def _rmsnorm_kernel(x_ref, g_ref, o_ref):
    x = x_ref[...]
    ms = jnp.mean(x * x, axis=-1, keepdims=True)
    o_ref[...] = x * lax.rsqrt(ms + EPS) * g_ref[...]


def final_norm(x, g, tm, n_out_blocks, in_block_of):
    d = x.shape[1]
    return pl.pallas_call(
        _rmsnorm_kernel,
        grid=(n_out_blocks,),
        in_specs=[
            pl.BlockSpec((tm, d), lambda t: (in_block_of(t), 0)),
            pl.BlockSpec((1, d), lambda t: (0, 0)),
        ],
        out_specs=pl.BlockSpec((tm, d), lambda t: (t, 0)),
        out_shape=jax.ShapeDtypeStruct((n_out_blocks * tm, d), F32),
        compiler_params=_params("arbitrary"),
        name="final_norm",
    )(x, g.reshape(1, d))


def _hgrn_kernel(*refs, layer, chunk, n_chunks, hb, zero_init, has_prev):
    refs = list(refs)
    zq_ref, zf_ref, zi_ref, zg_ref, lb_ref, on_ref = refs[:6]
    pos = 6
    s0_ref = None
    if not zero_init:
        s0_ref = refs[pos]
        pos += 1
    if has_prev:
        pos += 1
    y_ref, sfin_ref, s_ref, q_sc, k_sc, cum_sc, oint_sc, flag_ref = refs[pos:]
    c = chunk
    t = pl.program_id(2)

    @pl.when(t == 0)
    def _():
        if zero_init:
            s_ref[...] = jnp.zeros_like(s_ref)
        else:
            s_ref[...] = s0_ref[0]

    lbp = lb_ref[...]
    e = jnp.exp(lbp - jnp.max(lbp, axis=0, keepdims=True))
    p = e / jnp.sum(e, axis=0, keepdims=True)
    acc = p[0:1]
    for j in range(1, layer + 1):
        acc = acc + p[j:j + 1]
    lb = acc - p[0:1]
    log1m_lb = jnp.log1p(-lb)
    log_lb = jnp.log(jnp.maximum(lb, TINY))
    on_g = on_ref[...]

    ri = lax.broadcasted_iota(jnp.int32, (c, c), 0)
    ci_ = lax.broadcasted_iota(jnp.int32, (c, c), 1)
    tril = (ci_ <= ri).astype(BF16)
    row_c = lax.broadcasted_iota(jnp.int32, (c, LANES), 0)
    col_c = lax.broadcasted_iota(jnp.int32, (c, LANES), 1)
    causal = col_c <= row_c
    zpad_k = jnp.zeros((LANES - c, DK_A), F32)
    zpad_v = jnp.zeros((LANES - c, DV_A), F32)
    zpad_x = jnp.zeros((LANES - c - 8, DK_A), F32)

    def emit_y(o, rows, vs_):
        ms = jnp.mean(o * o, axis=-1, keepdims=True)
        o = o * lax.rsqrt(ms + EPS) * on_g
        g = zg_ref[rows, vs_]
        y_ref[rows, vs_] = (o * (g * _sigmoid(g))).astype(y_ref.dtype)

    for ci in range(n_chunks):
        rows = slice(ci * c, (ci + 1) * c)
        qz = zq_ref[rows, :]
        fz = zf_ref[rows, :]
        q = qz * _sigmoid(qz) * (DK_A ** -0.5)
        ef = jnp.exp(-jnp.abs(fz))
        ls = jnp.minimum(fz, 0.0) - jnp.log1p(ef)
        base = log1m_lb + ls
        lae = jnp.maximum(log_lb, base) + jnp.log1p(jnp.exp(-jnp.abs(log_lb - base)))
        logf = jnp.where(lb > 0, lae, base)
        k = (1.0 - lb) * (jnp.where(fz >= 0, ef, 1.0) / (1.0 + ef))

        h1 = logf.astype(BF16)
        r1 = logf - h1.astype(F32)
        h2 = r1.astype(BF16)
        h3 = (r1 - h2.astype(F32)).astype(BF16)
        cum = (jnp.dot(tril, h1, preferred_element_type=F32)
               + jnp.dot(tril, h2, preferred_element_type=F32)
               + jnp.dot(tril, h3, preferred_element_type=F32))
        last = cum[c - 1:c, :]
        mid = cum[c // 2 - 1:c // 2, :]
        flag_ref[ci] = (jnp.max(jnp.abs(cum - mid)) <= FACTOR_SAFE).astype(jnp.int32)
        q_sc[rows, :] = q
        k_sc[rows, :] = k
        cum_sc[rows, :] = cum

        for h in range(hb):
            hs = slice(h * DK_A, (h + 1) * DK_A)
            vs_ = slice(h * DV_A, (h + 1) * DV_A)
            q_h = q_sc[rows, hs]
            k_h = k_sc[rows, hs]
            cum_h = cum_sc[rows, hs]
            v_pad = jnp.concatenate([zi_ref[rows, vs_], zpad_v], axis=0).astype(BF16)
            d_h = cum_h - mid[:, hs]
            qm = (q_h * jnp.exp(d_h)).astype(BF16)
            km = jnp.concatenate([k_h * jnp.exp(-d_h), zpad_k], axis=0).astype(BF16)
            att = lax.dot_general(qm, km, (((1,), (1,)), ((), ())), preferred_element_type=F32)
            att = jnp.where(causal, att, 0.0).astype(BF16)
            o_intra = jnp.dot(att, v_pad, preferred_element_type=F32)
            s_prev = s_ref[h]
            qa = (q_h * jnp.exp(cum_h)).astype(BF16)
            o_inter = jnp.dot(qa, s_prev.astype(BF16), preferred_element_type=F32)
            oint_sc[rows, vs_] = o_inter
            kd = k_h * jnp.exp(last[:, hs] - cum_h)
            el8 = jnp.broadcast_to(jnp.exp(last[:, hs]), (8, DK_A))
            x_t = jnp.concatenate([kd, el8, zpad_x], axis=0).T
            s_ref[h] = x_t[:, c:c + 1] * s_prev + jnp.dot(x_t.astype(BF16), v_pad, preferred_element_type=F32)
            emit_y(o_inter + o_intra, rows, vs_)

    for ci in range(n_chunks):
        rows = slice(ci * c, (ci + 1) * c)

        @pl.when(flag_ref[ci] == 0)
        def _(rows=rows):
            row_v = lax.broadcasted_iota(jnp.int32, (c, 1), 0)

            def row_of(a, s):
                return jnp.sum(jnp.where(row_v == s, a, 0.0), axis=0, keepdims=True)

            for h in range(hb):
                hs = slice(h * DK_A, (h + 1) * DK_A)
                vs_ = slice(h * DV_A, (h + 1) * DV_A)
                q_h = q_sc[rows, hs]
                k_h = k_sc[rows, hs]
                cum_h = cum_sc[rows, hs]
                v_h = zi_ref[rows, vs_]

                def body(s, acc_o, q_h=q_h, k_h=k_h, cum_h=cum_h, v_h=v_h):
                    dec = jnp.exp(jnp.minimum(cum_h - row_of(cum_h, s), 0.0))
                    w = jnp.sum(q_h * row_of(k_h, s) * dec, axis=-1, keepdims=True)
                    w = jnp.where(row_v >= s, w, 0.0)
                    return acc_o + w * row_of(v_h, s)

                o_intra = lax.fori_loop(0, c, body, jnp.zeros((c, DV_A), F32))
                emit_y(oint_sc[rows, vs_] + o_intra, rows, vs_)

    @pl.when(t == pl.num_programs(2) - 1)
    def _():
        sfin_ref[0] = s_ref[...]


def hgrn_scan(z, lb_a, onorm_a, s0_all, states_prev, *, layer, n_batch, rows_per_batch, chunk, n_chunks, hb):
    n = z.shape[0]
    tb = chunk * n_chunks
    nt = rows_per_batch // tb
    ng = H_A // hb
    wk, wv = hb * DK_A, hb * DV_A
    f_off = (H_A * DK_A) // wk
    i_off = (2 * H_A * DK_A) // wv
    g_off = (2 * H_A * DK_A + E_WIDTH) // wv
    row = lambda b, g, t: b * nt + t
    in_specs = [
        pl.BlockSpec((tb, wk), lambda b, g, t: (row(b, g, t), g)),
        pl.BlockSpec((tb, wk), lambda b, g, t: (row(b, g, t), f_off + g)),
        pl.BlockSpec((tb, wv), lambda b, g, t: (row(b, g, t), i_off + g)),
        pl.BlockSpec((tb, wv), lambda b, g, t: (row(b, g, t), g_off + g)),
        pl.BlockSpec((N_A, wk), lambda b, g, t: (0, g)),
        pl.BlockSpec((None, 1, DV_A), lambda b, g, t: (layer, 0, 0)),
    ]
    args = [z, z, z, z, lb_a, onorm_a.reshape(N_A, 1, DV_A)]
    state_spec = pl.BlockSpec((None, 1, hb, DK_A, DV_A), lambda b, g, t: (layer, b, g, 0, 0))
    if s0_all is not None:
        in_specs.append(state_spec)
        args.append(s0_all)
    aliases = {}
    if states_prev is not None:
        aliases = {len(args): 1}
        in_specs.append(pl.BlockSpec(memory_space=pl.ANY))
        args.append(states_prev)
    kern = functools.partial(_hgrn_kernel, layer=layer, chunk=chunk, n_chunks=n_chunks, hb=hb,
                             zero_init=s0_all is None, has_prev=states_prev is not None)
    return pl.pallas_call(
        kern,
        grid=(n_batch, ng, nt),
        in_specs=in_specs,
        out_specs=[
            pl.BlockSpec((tb, wv), lambda b, g, t: (row(b, g, t), g)),
            state_spec,
        ],
        out_shape=[
            jax.ShapeDtypeStruct((n, E_WIDTH), _act_dtype(tb)),
            jax.ShapeDtypeStruct((N_A, n_batch, H_A, DK_A, DV_A), F32),
        ],
        scratch_shapes=[
            pltpu.VMEM((hb, DK_A, DV_A), F32),
            pltpu.VMEM((tb, wk), F32),
            pltpu.VMEM((tb, wk), F32),
            pltpu.VMEM((tb, wk), F32),
            pltpu.VMEM((tb, wv), F32),
            pltpu.SMEM((n_chunks,), jnp.int32),
        ],
        input_output_aliases=aliases,
        compiler_params=_params("arbitrary", "arbitrary", "arbitrary"),
        name=f"hgrn_scan_c{chunk}",
    )(*args)


def _swa_prompt_kernel(sink_ref, q_ref, g_ref, kc_ref, kp_ref, vc_ref, vp_ref, y_ref, d_sc, s_sc, p_sc):
    blk = pl.program_id(1)
    kg = pl.program_id(2)
    w = WINDOW

    @pl.when(kg == 0)
    def _():
        row = lax.broadcasted_iota(jnp.int32, (w, 2 * w), 0)
        col = lax.broadcasted_iota(jnp.int32, (w, 2 * w), 1)
        dist = w + row - col
        key_pos = (blk - 1) * w + col
        valid = (dist >= 0) & (dist < w) & (key_pos >= PAD)
        d_sc[...] = jnp.where(valid, dist.astype(F32), -NEG_BIG)

    lane = lax.broadcasted_iota(jnp.int32, (w, LANES), 1)
    row_k = lax.broadcasted_iota(jnp.int32, (2 * w, HD_B), 0)
    ones_v = jnp.ones((2 * w, HD_B), F32)
    for kk in range(KH_STEP):
        kh = kg * KH_STEP + kk
        kcat = jnp.concatenate([kp_ref[kk], kc_ref[kk]], axis=0).astype(BF16)
        vcat = jnp.concatenate([vp_ref[kk], vc_ref[kk]], axis=0)
        vext = jnp.concatenate([jnp.where(row_k == 0, 0.0, vcat), ones_v], axis=1).astype(BF16)
        qs = jnp.concatenate(
            [q_ref[:, (kk * G_B + g) * HD_B:(kk * G_B + g + 1) * HD_B] for g in range(G_B)], axis=0)
        qs = (qs * (HD_B ** -0.5)).astype(BF16)
        s_sc[kk] = lax.dot_general(qs, kcat, (((1,), (1,)), ((), ())), preferred_element_type=F32)
        for g in range(G_B):
            head = kh * G_B + g
            slope = jnp.exp(-LN2 * 8.0 * (head + 1).astype(F32) / H_B)
            sink = sink_ref[head]
            s_lo = s_sc[kk, g * w:(g + 1) * w, :w] - slope * d_sc[:, :w]
            s_lo = jnp.where(lane == 0, sink, s_lo)
            s_hi = s_sc[kk, g * w:(g + 1) * w, w:] - slope * d_sc[:, w:]
            m = jnp.max(jnp.maximum(s_lo, s_hi), axis=-1, keepdims=True)
            p_sc[kk, g * w:(g + 1) * w, :w] = jnp.exp(s_lo - m).astype(BF16)
            p_sc[kk, g * w:(g + 1) * w, w:] = jnp.exp(s_hi - m).astype(BF16)
        o_all = jnp.dot(p_sc[kk], vext, preferred_element_type=F32)
        for gp in range(G_B // 2):
            o_even = o_all[(2 * gp) * w:(2 * gp + 1) * w, :]
            o_odd = o_all[(2 * gp + 1) * w:(2 * gp + 2) * w, :]
            lo = o_even / pltpu.roll(o_even, HD_B, 1)
            hi = pltpu.roll(o_odd, HD_B, 1) / o_odd
            cs = slice((kk * G_B + 2 * gp) * HD_B, (kk * G_B + 2 * gp + 2) * HD_B)
            gate = g_ref[:, cs]
            y_ref[:, cs] = (jnp.where(lane < HD_B, lo, hi) * (gate * _sigmoid(gate))).astype(BF16)


def swa_prompt(zb, kv_hm, sink, n_batch, blocks_per_batch):
    n = zb.shape[0]
    w = WINDOW
    gw = KH_STEP * G_B * HD_B
    nb = blocks_per_batch
    nkg = KV_B // KH_STEP
    cur = lambda b, i, kg: b * nb + i
    prev = lambda b, i, kg: b * nb + jnp.maximum(i - 1, 0)
    return pl.pallas_call(
        _swa_prompt_kernel,
        grid=(n_batch, nb, nkg),
        in_specs=[
            pl.BlockSpec(memory_space=pltpu.SMEM),
            pl.BlockSpec((w, gw), lambda b, i, kg: (cur(b, i, kg), kg)),
            pl.BlockSpec((w, gw), lambda b, i, kg: (cur(b, i, kg), nkg + kg)),
            pl.BlockSpec((KH_STEP, w, HD_B), lambda b, i, kg: (kg, cur(b, i, kg), 0)),
            pl.BlockSpec((KH_STEP, w, HD_B), lambda b, i, kg: (kg, prev(b, i, kg), 0)),
            pl.BlockSpec((KH_STEP, w, HD_B), lambda b, i, kg: (nkg + kg, cur(b, i, kg), 0)),
            pl.BlockSpec((KH_STEP, w, HD_B), lambda b, i, kg: (nkg + kg, prev(b, i, kg), 0)),
        ],
        out_specs=pl.BlockSpec((w, gw), lambda b, i, kg: (cur(b, i, kg), kg)),
        out_shape=jax.ShapeDtypeStruct((n, E_WIDTH), BF16),
        scratch_shapes=[
            pltpu.VMEM((w, 2 * w), F32),
            pltpu.VMEM((KH_STEP, G_B * w, 2 * w), F32),
            pltpu.VMEM((KH_STEP, G_B * w, 2 * w), BF16),
        ],
        compiler_params=_params("arbitrary", "arbitrary", "arbitrary"),
        name="swa_prompt",
    )(sink, zb, zb, kv_hm, kv_hm, kv_hm, kv_hm)


def _swa_sample_kernel(sink_ref, q_ref, g_ref, kn_ref, vn_ref, ck_ref, cv_ref, y_ref, *, t_new):
    wb = ck_ref.shape[1]
    nk = wb + t_new
    nr = G_B * t_new
    row = lax.broadcasted_iota(jnp.int32, (nr, nk), 0)
    col = lax.broadcasted_iota(jnp.int32, (nr, nk), 1)
    tok = row % t_new
    grp = row // t_new
    dist = wb + tok - col
    valid = (dist >= 0) & (dist < WINDOW)
    distf = dist.astype(F32)
    grp1 = lax.broadcasted_iota(jnp.int32, (nr, 1), 0) // t_new
    for kh in range(KV_B):
        ks = slice(kh * HD_B, (kh + 1) * HD_B)
        kf = jnp.concatenate([ck_ref[0, :, ks], kn_ref[kh]], axis=0).astype(BF16)
        vf = jnp.concatenate([cv_ref[0, :, ks], vn_ref[kh]], axis=0).astype(BF16)
        qs = jnp.concatenate(
            [q_ref[:, (kh * G_B + g) * HD_B:(kh * G_B + g + 1) * HD_B] for g in range(G_B)], axis=0)
        s = lax.dot_general(qs.astype(BF16), kf, (((1,), (1,)), ((), ())), preferred_element_type=F32)
        slope = jnp.exp(-LN2 * 8.0 * (kh * G_B + grp + 1).astype(F32) / H_B)
        sink = jnp.zeros((nr, 1), F32)
        for g in range(G_B):
            sink = jnp.where(grp1 == g, sink_ref[kh * G_B + g], sink)
        s = s * (HD_B ** -0.5) - slope * distf
        s = jnp.where(valid, s, NEG_BIG)
        m = jnp.maximum(jnp.max(s, axis=-1, keepdims=True), sink)
        p = jnp.where(valid, jnp.exp(s - m), 0.0)
        den = jnp.sum(p, axis=-1, keepdims=True) + jnp.exp(sink - m)
        o = jnp.dot(p.astype(BF16), vf, preferred_element_type=F32) / den
        for g in range(G_B):
            hsl = slice((kh * G_B + g) * HD_B, (kh * G_B + g + 1) * HD_B)
            gate = g_ref[:, hsl]
            y_ref[:, hsl] = (o[g * t_new:(g + 1) * t_new, :] * (gate * _sigmoid(gate))).astype(y_ref.dtype)


def swa_sample(zb, kv_hm, sink, cache_k, cache_v, n_batch, t_new):
    n = zb.shape[0]
    wb = cache_k.shape[1]
    kern = functools.partial(_swa_sample_kernel, t_new=t_new)
    return pl.pallas_call(
        kern,
        grid=(n_batch,),
        in_specs=[
            pl.BlockSpec(memory_space=pltpu.SMEM),
            pl.BlockSpec((t_new, E_WIDTH), lambda b: (b, 0)),
            pl.BlockSpec((t_new, E_WIDTH), lambda b: (b, 1)),
            pl.BlockSpec((KV_B, t_new, HD_B), lambda b: (0, b, 0)),
            pl.BlockSpec((KV_B, t_new, HD_B), lambda b: (1, b, 0)),
            pl.BlockSpec((1, wb, KV_B * HD_B), lambda b: (b, 0, 0)),
            pl.BlockSpec((1, wb, KV_B * HD_B), lambda b: (b, 0, 0)),
        ],
        out_specs=pl.BlockSpec((t_new, E_WIDTH), lambda b: (b, 0)),
        out_shape=jax.ShapeDtypeStruct((n, E_WIDTH), _act_dtype(t_new)),
        compiler_params=_params("arbitrary"),
        name="swa_sample",
    )(sink, zb, zb, kv_hm, kv_hm, cache_k, cache_v)


def _trunk(x, s0_all, attend, weights, *, n_batch, rows_per_batch, chunk, n_chunks, hb):
    (norm_a, w_in_a, lb_a, onorm_a, w_out_a, norm_kv, w_kv, norm_b, w_in_b, sink_b, w_out_b) = weights
    states = None
    for layer in range(N_A):
        z = norm_matmul(x, norm_a, w_in_a, layer)
        y, states = hgrn_scan(z, lb_a, onorm_a, s0_all, states, layer=layer, n_batch=n_batch,
                              rows_per_batch=rows_per_batch, chunk=chunk, n_chunks=n_chunks, hb=hb)
        x = matmul_residual(y, w_out_a, layer, x)
    kv_hm = kv_proj(x, norm_kv, w_kv)
    for layer in range(N_B):
        zb = norm_matmul(x, norm_b, w_in_b, layer)
        y = attend(zb, kv_hm, sink_b[layer])
        x = matmul_residual(y, w_out_b, layer, x)
    return x, states, kv_hm


def kernel(x_prompt, x_sample, state_hgrn, cache_k, cache_v, meta_tokens, norm_a, w_in_a, lb_a, onorm_a,
           w_out_a, norm_kv, w_kv, norm_b, w_in_b, sink_b, w_out_b, norm_f):
    bsz, seq, d = x_prompt.shape
    dec_b, dec_t, _ = x_sample.shape
    wb = cache_k.shape[1]
    weights = (norm_a, w_in_a.astype(BF16), lb_a, onorm_a, w_out_a.astype(BF16), norm_kv, w_kv.astype(BF16),
               norm_b, w_in_b.astype(BF16), sink_b, w_out_b.astype(BF16))

    rows_p = PAD + N_META + seq
    nb_p = rows_p // ROW_BLOCK
    xp = jnp.concatenate([
        jnp.zeros((bsz, PAD, d), x_prompt.dtype),
        jnp.broadcast_to(meta_tokens.astype(x_prompt.dtype)[None], (bsz, N_META, d)),
        x_prompt], axis=1).reshape(bsz * rows_p, d)
    attend_p = functools.partial(swa_prompt, n_batch=bsz, blocks_per_batch=nb_p)
    xp, sp, kvp = _trunk(xp, None, attend_p, weights, n_batch=bsz, rows_per_batch=rows_p, chunk=64,
                         n_chunks=ROW_BLOCK // 64, hb=8)
    blocks_out = seq // ROW_BLOCK
    y_prompt = final_norm(xp, norm_f, ROW_BLOCK, bsz * blocks_out,
                          lambda t: (t // blocks_out) * nb_p + 1 + t % blocks_out).reshape(bsz, seq, d)
    kvp = kvp.reshape(2, KV_B, bsz, rows_p, HD_B)[:, :, :, rows_p - wb:, :]
    cache_k_prompt = jnp.transpose(kvp[0], (1, 2, 0, 3)).astype(cache_k.dtype)
    cache_v_prompt = jnp.transpose(kvp[1], (1, 2, 0, 3)).astype(cache_v.dtype)

    xs = x_sample.reshape(dec_b * dec_t, d)
    attend_s = functools.partial(swa_sample, cache_k=cache_k.reshape(dec_b, wb, KV_B * HD_B),
                                 cache_v=cache_v.reshape(dec_b, wb, KV_B * HD_B), n_batch=dec_b, t_new=dec_t)
    xs, ss, kvs = _trunk(xs, state_hgrn.astype(F32), attend_s, weights, n_batch=dec_b, rows_per_batch=dec_t,
                         chunk=dec_t, n_chunks=1, hb=8)
    y_sample = final_norm(xs, norm_f, dec_b * dec_t, 1, lambda t: t).reshape(dec_b, dec_t, d)
    kvs = kvs.reshape(2, KV_B, dec_b, dec_t, HD_B)
    k_new = jnp.transpose(kvs[0], (1, 2, 0, 3)).astype(cache_k.dtype)
    v_new = jnp.transpose(kvs[1], (1, 2, 0, 3)).astype(cache_v.dtype)
    cache_k_sample = jnp.concatenate([cache_k, k_new], axis=1)[:, -wb:]
    cache_v_sample = jnp.concatenate([cache_v, v_new], axis=1)[:, -wb:]

    return (y_prompt, y_sample, sp.astype(state_hgrn.dtype), cache_k_prompt, cache_v_prompt,
            ss.astype(state_hgrn.dtype), cache_k_sample, cache_v_sample)
```

```python
import functools

import jax
import jax.numpy as jnp
from jax import lax
from jax.experimental import pallas as pl
from jax.experimental.pallas import tpu as pltpu

D_MODEL = 2048
E_WIDTH = 2 * D_MODEL
N_A = 2
N_B = 2
DK_A = 128
H_A = D_MODEL // DK_A
DV_A = E_WIDTH // H_A
HD_B = 64
H_B = E_WIDTH // HD_B
KV_B = H_B // 8
G_B = H_B // KV_B
WINDOW = 128
N_META = 16
EPS = 1e-6
NEG_BIG = -1e30
TINY = 1e-30

LANES = 128
BF16_SUBLANES = 16
ROW_BLOCK = 128
PAD = ROW_BLOCK - N_META
FACTOR_SAFE = 60.0
LN2 = 0.6931471805599453
VMEM_LIMIT = 56 * 1024 * 1024
KH_STEP = 2

F32 = jnp.float32
BF16 = jnp.bfloat16


def _sigmoid(x):
    return 1.0 / (1.0 + jnp.exp(-x))


def _params(*sem):
    return pltpu.CompilerParams(dimension_semantics=sem, vmem_limit_bytes=VMEM_LIMIT)


def _row_tile(n, cap):
    for t in range(min(cap, n) // 8 * 8, 0, -8):
        if n % t == 0:
            return t
    raise ValueError(f"no row tile for {n}")


def _act_dtype(rows_per_block):
    return BF16 if rows_per_block % BF16_SUBLANES == 0 else F32


def _norm_matmul_kernel(x_ref, g_ref, w_ref, o_ref, xn_ref):
    @pl.when(pl.program_id(1) == 0)
    def _():
        x = x_ref[...]
        ms = jnp.mean(x * x, axis=-1, keepdims=True)
        xn_ref[...] = (x * lax.rsqrt(ms + EPS) * g_ref[...]).astype(BF16)

    o_ref[...] = jnp.dot(xn_ref[...], w_ref[...], preferred_element_type=F32).astype(o_ref.dtype)


def norm_matmul(x, g_all, w_all, layer, col0, ncols, out_dtype, tn=1024):
    n, d = x.shape
    nl = w_all.shape[0]
    tm = _row_tile(n, 1088)
    jb = col0 // tn
    return pl.pallas_call(
        _norm_matmul_kernel,
        grid=(n // tm, ncols // tn),
        in_specs=[
            pl.BlockSpec((tm, d), lambda i, j: (i, 0)),
            pl.BlockSpec((None, 1, d), lambda i, j: (layer, 0, 0)),
            pl.BlockSpec((None, d, tn), lambda i, j: (layer, 0, jb + j)),
        ],
        out_specs=pl.BlockSpec((tm, tn), lambda i, j: (i, j)),
        out_shape=jax.ShapeDtypeStruct((n, ncols), out_dtype),
        scratch_shapes=[pltpu.VMEM((tm, d), BF16)],
        compiler_params=_params("arbitrary", "arbitrary"),
        name="norm_matmul",
    )(x, g_all.reshape(nl, 1, d), w_all)


def _kv_proj_kernel(x_ref, g_ref, w_ref, o_ref):
    x = x_ref[...]
    ms = jnp.mean(x * x, axis=-1, keepdims=True)
    xn = (x * lax.rsqrt(ms + EPS) * g_ref[...]).astype(BF16)
    kv = jnp.dot(xn, w_ref[...], preferred_element_type=F32)
    ones = jnp.ones((kv.shape[0], HD_B), F32)
    for j in range(KV_B):
        k_j = kv[:, j * HD_B:(j + 1) * HD_B]
        o_ref[j] = jnp.concatenate([k_j, k_j], axis=1)
        v_j = kv[:, (KV_B + j) * HD_B:(KV_B + j + 1) * HD_B]
        o_ref[KV_B + j] = jnp.concatenate([v_j, ones], axis=1)


def kv_proj(x, g, w):
    n, d = x.shape
    tm = _row_tile(n, 544)
    return pl.pallas_call(
        _kv_proj_kernel,
        grid=(n // tm,),
        in_specs=[
            pl.BlockSpec((tm, d), lambda i: (i, 0)),
            pl.BlockSpec((1, d), lambda i: (0, 0)),
            pl.BlockSpec((d, 2 * KV_B * HD_B), lambda i: (0, 0)),
        ],
        out_specs=pl.BlockSpec((2 * KV_B, tm, 2 * HD_B), lambda i: (0, i, 0)),
        out_shape=jax.ShapeDtypeStruct((2 * KV_B, n, 2 * HD_B), F32),
        compiler_params=_params("arbitrary"),
        name="kv_proj",
    )(x, g.reshape(1, d), w)


def _matmul_res_kernel(y_ref, w_ref, x_ref, o_ref):
    o_ref[...] = x_ref[...] + jnp.dot(y_ref[...].astype(BF16), w_ref[...], preferred_element_type=F32)


def matmul_residual(y, w_all, layer, x):
    n, e = y.shape
    d = w_all.shape[2]
    tm = _row_tile(n, 544)
    return pl.pallas_call(
        _matmul_res_kernel,
        grid=(n // tm,),
        in_specs=[
            pl.BlockSpec((tm, e), lambda i: (i, 0)),
            pl.BlockSpec((None, e, d), lambda i: (layer, 0, 0), pipeline_mode=pl.Buffered(1)),
            pl.BlockSpec((tm, d), lambda i: (i, 0)),
        ],
        out_specs=pl.BlockSpec((tm, d), lambda i: (i, 0)),
        out_shape=jax.ShapeDtypeStruct((n, d), F32),
        compiler_params=_params("arbitrary"),
        name="matmul_residual",
    )(y, w_all, x)


def _rmsnorm_kernel(x_ref, g_ref, o_ref):
    x = x_ref[...]
    ms = jnp.mean(x * x, axis=-1, keepdims=True)
    o_ref[...] = x * lax.rsqrt(ms + EPS) * g_ref[...]


def final_norm(x, g, tm, n_out_blocks, in_block_of):
    d = x.shape[1]
    return pl.pallas_call(
        _rmsnorm_kernel,
        grid=(n_out_blocks,),
        in_specs=[
            pl.BlockSpec((tm, d), lambda t: (in_block_of(t), 0)),
            pl.BlockSpec((1, d), lambda t: (0, 0)),
        ],
        out_specs=pl.BlockSpec((tm, d), lambda t: (t, 0)),
        out_shape=jax.ShapeDtypeStruct((n_out_blocks * tm, d), F32),
        compiler_params=_params("arbitrary"),
        name="final_norm",
    )(x, g.reshape(1, d))


def _hgrn_kernel(*refs, layer, chunk, n_chunks, hb, zero_init, has_prev):
    refs = list(refs)
    zq_ref, zf_ref, zi_ref, zg_ref, lb_ref, on_ref = refs[:6]
    pos = 6
    s0_ref = None
    if not zero_init:
        s0_ref = refs[pos]
        pos += 1
    if has_prev:
        pos += 1
    y_ref, sfin_ref, s_ref, q_sc, k_sc, cum_sc, oint_sc, flag_ref = refs[pos:]
    c = chunk
    t = pl.program_id(2)

    @pl.when(t == 0)
    def _():
        if zero_init:
            s_ref[...] = jnp.zeros_like(s_ref)
        else:
            s_ref[...] = s0_ref[0]

    lbp = lb_ref[...]
    e = jnp.exp(lbp - jnp.max(lbp, axis=0, keepdims=True))
    p = e / jnp.sum(e, axis=0, keepdims=True)
    acc = p[0:1]
    for j in range(1, layer + 1):
        acc = acc + p[j:j + 1]
    lb = acc - p[0:1]
    log1m_lb = jnp.log1p(-lb)
    log_lb = jnp.log(jnp.maximum(lb, TINY))
    on_g = on_ref[...]

    ri = lax.broadcasted_iota(jnp.int32, (c, c), 0)
    ci_ = lax.broadcasted_iota(jnp.int32, (c, c), 1)
    tril = (ci_ <= ri).astype(BF16)
    row_c = lax.broadcasted_iota(jnp.int32, (c, LANES), 0)
    col_c = lax.broadcasted_iota(jnp.int32, (c, LANES), 1)
    causal = col_c <= row_c
    zpad_k = jnp.zeros((LANES - c, DK_A), F32)
    zpad_v = jnp.zeros((LANES - c, DV_A), BF16)
    zpad_x = jnp.zeros((LANES - c - 8, DK_A), F32)

    def emit_y(o, rows, vs_):
        ms = jnp.mean(o * o, axis=-1, keepdims=True)
        o = o * lax.rsqrt(ms + EPS) * on_g
        g = zg_ref[rows, vs_].astype(F32)
        y_ref[rows, vs_] = (o * (g * _sigmoid(g))).astype(y_ref.dtype)

    for ci in range(n_chunks):
        rows = slice(ci * c, (ci + 1) * c)
        qz = zq_ref[rows, :]
        fz = zf_ref[rows, :]
        q = qz * _sigmoid(qz) * (DK_A ** -0.5)
        ef = jnp.exp(-jnp.abs(fz))
        ls = jnp.minimum(fz, 0.0) - jnp.log1p(ef)
        base = log1m_lb + ls
        lae = jnp.maximum(log_lb, base) + jnp.log1p(jnp.exp(-jnp.abs(log_lb - base)))
        logf = jnp.where(lb > 0, lae, base)
        k = (1.0 - lb) * (jnp.where(fz >= 0, ef, 1.0) / (1.0 + ef))

        h1 = logf.astype(BF16)
        r1 = logf - h1.astype(F32)
        h2 = r1.astype(BF16)
        h3 = (r1 - h2.astype(F32)).astype(BF16)
        cum = (jnp.dot(tril, h1, preferred_element_type=F32)
               + jnp.dot(tril, h2, preferred_element_type=F32)
               + jnp.dot(tril, h3, preferred_element_type=F32))
        last = cum[c - 1:c, :]
        mid = cum[c // 2 - 1:c // 2, :]
        flag_ref[ci] = (jnp.max(jnp.abs(cum - mid)) <= FACTOR_SAFE).astype(jnp.int32)
        q_sc[rows, :] = q
        k_sc[rows, :] = k
        cum_sc[rows, :] = cum

        for h in range(hb):
            hs = slice(h * DK_A, (h + 1) * DK_A)
            vs_ = slice(h * DV_A, (h + 1) * DV_A)
            q_h = q_sc[rows, hs]
            k_h = k_sc[rows, hs]
            cum_h = cum_sc[rows, hs]
            v_pad = jnp.concatenate([zi_ref[rows, vs_].astype(BF16), zpad_v], axis=0)
            d_h = cum_h - mid[:, hs]
            qm = (q_h * jnp.exp(d_h)).astype(BF16)
            km = jnp.concatenate([k_h * jnp.exp(-d_h), zpad_k], axis=0).astype(BF16)
            att = lax.dot_general(qm, km, (((1,), (1,)), ((), ())), preferred_element_type=F32)
            att = jnp.where(causal, att, 0.0).astype(BF16)
            o_intra = jnp.dot(att, v_pad, preferred_element_type=F32)
            s_prev = s_ref[h]
            qa = (q_h * jnp.exp(cum_h)).astype(BF16)
            o_inter = jnp.dot(qa, s_prev.astype(BF16), preferred_element_type=F32)
            oint_sc[rows, vs_] = o_inter
            kd = k_h * jnp.exp(last[:, hs] - cum_h)
            el8 = jnp.broadcast_to(jnp.exp(last[:, hs]), (8, DK_A))
            x_t = jnp.concatenate([kd, el8, zpad_x], axis=0).T
            s_ref[h] = x_t[:, c:c + 1] * s_prev + jnp.dot(x_t.astype(BF16), v_pad, preferred_element_type=F32)
            emit_y(o_inter + o_intra, rows, vs_)

    for ci in range(n_chunks):
        rows = slice(ci * c, (ci + 1) * c)

        @pl.when(flag_ref[ci] == 0)
        def _(rows=rows):
            row_v = lax.broadcasted_iota(jnp.int32, (c, 1), 0)

            def row_of(a, s):
                return jnp.sum(jnp.where(row_v == s, a, 0.0), axis=0, keepdims=True)

            for h in range(hb):
                hs = slice(h * DK_A, (h + 1) * DK_A)
                vs_ = slice(h * DV_A, (h + 1) * DV_A)
                q_h = q_sc[rows, hs]
                k_h = k_sc[rows, hs]
                cum_h = cum_sc[rows, hs]
                v_h = zi_ref[rows, vs_].astype(F32)

                def body(s, acc_o, q_h=q_h, k_h=k_h, cum_h=cum_h, v_h=v_h):
                    dec = jnp.exp(jnp.minimum(cum_h - row_of(cum_h, s), 0.0))
                    w = jnp.sum(q_h * row_of(k_h, s) * dec, axis=-1, keepdims=True)
                    w = jnp.where(row_v >= s, w, 0.0)
                    return acc_o + w * row_of(v_h, s)

                o_intra = lax.fori_loop(0, c, body, jnp.zeros((c, DV_A), F32))
                emit_y(oint_sc[rows, vs_] + o_intra, rows, vs_)

    @pl.when(t == pl.num_programs(2) - 1)
    def _():
        sfin_ref[0] = s_ref[...]


def hgrn_scan(zqf, zig, lb_a, onorm_a, s0_all, states_prev, *, layer, n_batch, rows_per_batch, chunk, n_chunks, hb):
    n = zqf.shape[0]
    tb = chunk * n_chunks
    nt = rows_per_batch // tb
    ng = H_A // hb
    wk, wv = hb * DK_A, hb * DV_A
    f_off = (H_A * DK_A) // wk
    g_off = E_WIDTH // wv
    row = lambda b, g, t: b * nt + t
    in_specs = [
        pl.BlockSpec((tb, wk), lambda b, g, t: (row(b, g, t), g)),
        pl.BlockSpec((tb, wk), lambda b, g, t: (row(b, g, t), f_off + g)),
        pl.BlockSpec((tb, wv), lambda b, g, t: (row(b, g, t), g)),
        pl.BlockSpec((tb, wv), lambda b, g, t: (row(b, g, t), g_off + g)),
        pl.BlockSpec((N_A, wk), lambda b, g, t: (0, g)),
        pl.BlockSpec((None, 1, DV_A), lambda b, g, t: (layer, 0, 0)),
    ]
    args = [zqf, zqf, zig, zig, lb_a, onorm_a.reshape(N_A, 1, DV_A)]
    state_spec = pl.BlockSpec((None, 1, hb, DK_A, DV_A), lambda b, g, t: (layer, b, g, 0, 0))
    if s0_all is not None:
        in_specs.append(state_spec)
        args.append(s0_all)
    aliases = {}
    if states_prev is not None:
        aliases = {len(args): 1}
        in_specs.append(pl.BlockSpec(memory_space=pl.ANY))
        args.append(states_prev)
    kern = functools.partial(_hgrn_kernel, layer=layer, chunk=chunk, n_chunks=n_chunks, hb=hb,
                             zero_init=s0_all is None, has_prev=states_prev is not None)
    return pl.pallas_call(
        kern,
        grid=(n_batch, ng, nt),
        in_specs=in_specs,
        out_specs=[
            pl.BlockSpec((tb, wv), lambda b, g, t: (row(b, g, t), g)),
            state_spec,
        ],
        out_shape=[
            jax.ShapeDtypeStruct((n, E_WIDTH), _act_dtype(tb)),
            jax.ShapeDtypeStruct((N_A, n_batch, H_A, DK_A, DV_A), F32),
        ],
        scratch_shapes=[
            pltpu.VMEM((hb, DK_A, DV_A), F32),
            pltpu.VMEM((tb, wk), F32),
            pltpu.VMEM((tb, wk), F32),
            pltpu.VMEM((tb, wk), F32),
            pltpu.VMEM((tb, wv), F32),
            pltpu.SMEM((n_chunks,), jnp.int32),
        ],
        input_output_aliases=aliases,
        compiler_params=_params("arbitrary", "arbitrary", "arbitrary"),
        name=f"hgrn_scan_c{chunk}",
    )(*args)


def _swa_prompt_kernel(sink_ref, q_ref, g_ref, kc_ref, kp_ref, vc_ref, vp_ref, y_ref, d_sc, s_sc, p_sc):
    blk = pl.program_id(1)
    kg = pl.program_id(2)
    w = WINDOW
    npair = G_B // 2

    @pl.when(kg == 0)
    def _():
        row = lax.broadcasted_iota(jnp.int32, (w, 2 * w), 0)
        col = lax.broadcasted_iota(jnp.int32, (w, 2 * w), 1)
        dist = w + row - col
        key_pos = (blk - 1) * w + col
        valid = (dist >= 0) & (dist < w) & (key_pos >= PAD)
        d_sc[...] = jnp.where(valid, dist.astype(F32), -NEG_BIG)

    lane = lax.broadcasted_iota(jnp.int32, (w, LANES), 1)
    lane_k = lax.broadcasted_iota(jnp.int32, (2 * w, LANES), 1)
    row_k = lax.broadcasted_iota(jnp.int32, (2 * w, LANES), 0)
    for kk in range(KH_STEP):
        kh = kg * KH_STEP + kk
        kdup = jnp.concatenate([kp_ref[kk], kc_ref[kk]], axis=0)
        k_par = [jnp.where(lane_k < HD_B, kdup, 0.0).astype(BF16), jnp.where(lane_k < HD_B, 0.0, kdup).astype(BF16)]
        vraw = jnp.concatenate([vp_ref[kk], vc_ref[kk]], axis=0)
        vext = jnp.where(row_k == 0, jnp.where(lane_k < HD_B, 0.0, vraw), vraw).astype(BF16)
        qs = jnp.concatenate(
            [q_ref[:, (kk * npair + j) * LANES:(kk * npair + j + 1) * LANES] for j in range(npair)], axis=0)
        qs = (qs * (HD_B ** -0.5)).astype(BF16)
        for par in range(2):
            s_sc[kk, par] = lax.dot_general(qs, k_par[par], (((1,), (1,)), ((), ())),
                                            preferred_element_type=F32)
        for j in range(npair):
            rows = slice(j * w, (j + 1) * w)
            for par in range(2):
                head = kh * G_B + 2 * j + par
                slope = jnp.exp(-LN2 * 8.0 * (head + 1).astype(F32) / H_B)
                sink = sink_ref[head]
                s_lo = s_sc[kk, par, rows, :w] - slope * d_sc[:, :w]
                s_lo = jnp.where(lane == 0, sink, s_lo)
                s_hi = s_sc[kk, par, rows, w:] - slope * d_sc[:, w:]
                m = jnp.max(jnp.maximum(s_lo, s_hi), axis=-1, keepdims=True)
                p_sc[kk, par, rows, :w] = jnp.exp(s_lo - m).astype(BF16)
                p_sc[kk, par, rows, w:] = jnp.exp(s_hi - m).astype(BF16)
        o_par = [jnp.dot(p_sc[kk, par], vext, preferred_element_type=F32) for par in range(2)]
        for j in range(npair):
            o_even = o_par[0][j * w:(j + 1) * w, :]
            o_odd = o_par[1][j * w:(j + 1) * w, :]
            lo = o_even / pltpu.roll(o_even, HD_B, 1)
            hi = pltpu.roll(o_odd, HD_B, 1) / o_odd
            cs = slice((kk * npair + j) * LANES, (kk * npair + j + 1) * LANES)
            gate = g_ref[:, cs].astype(F32)
            y_ref[:, cs] = (jnp.where(lane < HD_B, lo, hi) * (gate * _sigmoid(gate))).astype(BF16)


def swa_prompt(zb, kv_hm, sink, n_batch, blocks_per_batch):
    n = zb.shape[0]
    w = WINDOW
    gw = KH_STEP * G_B * HD_B
    nb = blocks_per_batch
    nkg = KV_B // KH_STEP
    cur = lambda b, i, kg: b * nb + i
    prev = lambda b, i, kg: b * nb + jnp.maximum(i - 1, 0)
    kv_block = (KH_STEP, w, 2 * HD_B)
    return pl.pallas_call(
        _swa_prompt_kernel,
        grid=(n_batch, nb, nkg),
        in_specs=[
            pl.BlockSpec(memory_space=pltpu.SMEM),
            pl.BlockSpec((w, gw), lambda b, i, kg: (cur(b, i, kg), kg)),
            pl.BlockSpec((w, gw), lambda b, i, kg: (cur(b, i, kg), nkg + kg)),
            pl.BlockSpec(kv_block, lambda b, i, kg: (kg, cur(b, i, kg), 0)),
            pl.BlockSpec(kv_block, lambda b, i, kg: (kg, prev(b, i, kg), 0)),
            pl.BlockSpec(kv_block, lambda b, i, kg: (nkg + kg, cur(b, i, kg), 0)),
            pl.BlockSpec(kv_block, lambda b, i, kg: (nkg + kg, prev(b, i, kg), 0)),
        ],
        out_specs=pl.BlockSpec((w, gw), lambda b, i, kg: (cur(b, i, kg), kg)),
        out_shape=jax.ShapeDtypeStruct((n, E_WIDTH), BF16),
        scratch_shapes=[
            pltpu.VMEM((w, 2 * w), F32),
            pltpu.VMEM((KH_STEP, 2, G_B // 2 * w, 2 * w), F32),
            pltpu.VMEM((KH_STEP, 2, G_B // 2 * w, 2 * w), BF16),
        ],
        compiler_params=_params("arbitrary", "arbitrary", "arbitrary"),
        name="swa_prompt",
    )(sink, zb, zb, kv_hm, kv_hm, kv_hm, kv_hm)


def _swa_sample_kernel(sink_ref, q_ref, g_ref, kn_ref, vn_ref, ck_ref, cv_ref, y_ref, *, t_new):
    wb = ck_ref.shape[1]
    nk = wb + t_new
    nr = G_B * t_new
    row = lax.broadcasted_iota(jnp.int32, (nr, nk), 0)
    col = lax.broadcasted_iota(jnp.int32, (nr, nk), 1)
    tok = row % t_new
    grp = row // t_new
    dist = wb + tok - col
    valid = (dist >= 0) & (dist < WINDOW)
    distf = dist.astype(F32)
    grp1 = lax.broadcasted_iota(jnp.int32, (nr, 1), 0) // t_new
    for kh in range(KV_B):
        ks = slice(kh * HD_B, (kh + 1) * HD_B)
        kf = jnp.concatenate([ck_ref[0, :, ks], kn_ref[kh][:, :HD_B]], axis=0).astype(BF16)
        vf = jnp.concatenate([cv_ref[0, :, ks], vn_ref[kh][:, :HD_B]], axis=0).astype(BF16)
        qs = jnp.concatenate(
            [q_ref[:, (kh * G_B + g) * HD_B:(kh * G_B + g + 1) * HD_B] for g in range(G_B)], axis=0)
        s = lax.dot_general(qs.astype(BF16), kf, (((1,), (1,)), ((), ())), preferred_element_type=F32)
        slope = jnp.exp(-LN2 * 8.0 * (kh * G_B + grp + 1).astype(F32) / H_B)
        sink = jnp.zeros((nr, 1), F32)
        for g in range(G_B):
            sink = jnp.where(grp1 == g, sink_ref[kh * G_B + g], sink)
        s = s * (HD_B ** -0.5) - slope * distf
        s = jnp.where(valid, s, NEG_BIG)
        m = jnp.maximum(jnp.max(s, axis=-1, keepdims=True), sink)
        p = jnp.where(valid, jnp.exp(s - m), 0.0)
        den = jnp.sum(p, axis=-1, keepdims=True) + jnp.exp(sink - m)
        o = jnp.dot(p.astype(BF16), vf, preferred_element_type=F32) / den
        for g in range(G_B):
            hsl = slice((kh * G_B + g) * HD_B, (kh * G_B + g + 1) * HD_B)
            gate = g_ref[:, hsl].astype(F32)
            y_ref[:, hsl] = (o[g * t_new:(g + 1) * t_new, :] * (gate * _sigmoid(gate))).astype(y_ref.dtype)


def swa_sample(zb, kv_hm, sink, cache_k, cache_v, n_batch, t_new):
    n = zb.shape[0]
    wb = cache_k.shape[1]
    kern = functools.partial(_swa_sample_kernel, t_new=t_new)
    return pl.pallas_call(
        kern,
        grid=(n_batch,),
        in_specs=[
            pl.BlockSpec(memory_space=pltpu.SMEM),
            pl.BlockSpec((t_new, E_WIDTH), lambda b: (b, 0)),
            pl.BlockSpec((t_new, E_WIDTH), lambda b: (b, 1)),
            pl.BlockSpec((KV_B, t_new, 2 * HD_B), lambda b: (0, b, 0)),
            pl.BlockSpec((KV_B, t_new, 2 * HD_B), lambda b: (1, b, 0)),
            pl.BlockSpec((1, wb, KV_B * HD_B), lambda b: (b, 0, 0)),
            pl.BlockSpec((1, wb, KV_B * HD_B), lambda b: (b, 0, 0)),
        ],
        out_specs=pl.BlockSpec((t_new, E_WIDTH), lambda b: (b, 0)),
        out_shape=jax.ShapeDtypeStruct((n, E_WIDTH), _act_dtype(t_new)),
        compiler_params=_params("arbitrary"),
        name="swa_sample",
    )(sink, zb, zb, kv_hm, kv_hm, cache_k, cache_v)


def _trunk(x, s0_all, attend, weights, *, n_batch, rows_per_batch, chunk, n_chunks, hb):
    (norm_a, w_in_a, lb_a, onorm_a, w_out_a, norm_kv, w_kv, norm_b, w_in_b, sink_b, w_out_b) = weights
    fd = 2 * H_A * DK_A
    act = _act_dtype(chunk * n_chunks)
    states = None
    for layer in range(N_A):
        zqf = norm_matmul(x, norm_a, w_in_a, layer, 0, fd, F32)
        zig = norm_matmul(x, norm_a, w_in_a, layer, fd, 2 * E_WIDTH, act)
        y, states = hgrn_scan(zqf, zig, lb_a, onorm_a, s0_all, states, layer=layer, n_batch=n_batch,
                              rows_per_batch=rows_per_batch, chunk=chunk, n_chunks=n_chunks, hb=hb)
        x = matmul_residual(y, w_out_a, layer, x)
    kv_hm = kv_proj(x, norm_kv, w_kv)
    for layer in range(N_B):
        zb = norm_matmul(x, norm_b, w_in_b, layer, 0, 2 * E_WIDTH, act)
        y = attend(zb, kv_hm, sink_b[layer])
        x = matmul_residual(y, w_out_b, layer, x)
    return x, states, kv_hm


def kernel(x_prompt, x_sample, state_hgrn, cache_k, cache_v, meta_tokens, norm_a, w_in_a, lb_a, onorm_a,
           w_out_a, norm_kv, w_kv, norm_b, w_in_b, sink_b, w_out_b, norm_f):
    bsz, seq, d = x_prompt.shape
    dec_b, dec_t, _ = x_sample.shape
    wb = cache_k.shape[1]
    weights = (norm_a, w_in_a.astype(BF16), lb_a, onorm_a, w_out_a.astype(BF16), norm_kv, w_kv.astype(BF16),
               norm_b, w_in_b.astype(BF16), sink_b, w_out_b.astype(BF16))

    rows_p = PAD + N_META + seq
    nb_p = rows_p // ROW_BLOCK
    xp = jnp.concatenate([
        jnp.zeros((bsz, PAD, d), x_prompt.dtype),
        jnp.broadcast_to(meta_tokens.astype(x_prompt.dtype)[None], (bsz, N_META, d)),
        x_prompt], axis=1).reshape(bsz * rows_p, d)
    attend_p = functools.partial(swa_prompt, n_batch=bsz, blocks_per_batch=nb_p)
    xp, sp, kvp = _trunk(xp, None, attend_p, weights, n_batch=bsz, rows_per_batch=rows_p, chunk=64,
                         n_chunks=ROW_BLOCK // 64, hb=8)
    blocks_out = seq // ROW_BLOCK
    y_prompt = final_norm(xp, norm_f, ROW_BLOCK, bsz * blocks_out,
                          lambda t: (t // blocks_out) * nb_p + 1 + t % blocks_out).reshape(bsz, seq, d)
    kvp = kvp.reshape(2, KV_B, bsz, rows_p, 2 * HD_B)[:, :, :, rows_p - wb:, :HD_B]
    cache_k_prompt = jnp.transpose(kvp[0], (1, 2, 0, 3)).astype(cache_k.dtype)
    cache_v_prompt = jnp.transpose(kvp[1], (1, 2, 0, 3)).astype(cache_v.dtype)

    xs = x_sample.reshape(dec_b * dec_t, d)
    attend_s = functools.partial(swa_sample, cache_k=cache_k.reshape(dec_b, wb, KV_B * HD_B),
                                 cache_v=cache_v.reshape(dec_b, wb, KV_B * HD_B), n_batch=dec_b, t_new=dec_t)
    xs, ss, kvs = _trunk(xs, state_hgrn.astype(F32), attend_s, weights, n_batch=dec_b, rows_per_batch=dec_t,
                         chunk=dec_t, n_chunks=1, hb=8)
    y_sample = final_norm(xs, norm_f, dec_b * dec_t, 1, lambda t: t).reshape(dec_b, dec_t, d)
    kvs = kvs.reshape(2, KV_B, dec_b, dec_t, 2 * HD_B)[..., :HD_B]
    k_new = jnp.transpose(kvs[0], (1, 2, 0, 3)).astype(cache_k.dtype)
    v_new = jnp.transpose(kvs[1], (1, 2, 0, 3)).astype(cache_v.dtype)
    cache_k_sample = jnp.concatenate([cache_k, k_new], axis=1)[:, -wb:]
    cache_v_sample = jnp.concatenate([cache_v, v_new], axis=1)[:, -wb:]

    return (y_prompt, y_sample, sp.astype(state_hgrn.dtype), cache_k_prompt, cache_v_prompt,
            ss.astype(state_hgrn.dtype), cache_k_sample, cache_v_sample)
```

```python
import functools

import jax
import jax.numpy as jnp
from jax import lax
from jax.experimental import pallas as pl
from jax.experimental.pallas import tpu as pltpu

D_MODEL = 2048
E_WIDTH = 2 * D_MODEL
N_A = 2
N_B = 2
DK_A = 128
H_A = D_MODEL // DK_A
DV_A = E_WIDTH // H_A
HD_B = 64
H_B = E_WIDTH // HD_B
KV_B = H_B // 8
G_B = H_B // KV_B
WINDOW = 128
N_META = 16
EPS = 1e-6
NEG_BIG = -1e30
TINY = 1e-30

LANES = 128
BF16_SUBLANES = 16
ROW_BLOCK = 128
PAD = ROW_BLOCK - N_META
FACTOR_SAFE = 60.0
LN2 = 0.6931471805599453
VMEM_LIMIT = 56 * 1024 * 1024
KH_STEP = 2

F32 = jnp.float32
BF16 = jnp.bfloat16


def _sigmoid(x):
    return 1.0 / (1.0 + jnp.exp(-x))


def _params(*sem):
    return pltpu.CompilerParams(dimension_semantics=sem, vmem_limit_bytes=VMEM_LIMIT)


def _row_tile(n, cap):
    for t in range(min(cap, n) // 8 * 8, 0, -8):
        if n % t == 0:
            return t
    raise ValueError(f"no row tile for {n}")


def _act_dtype(rows_per_block):
    return BF16 if rows_per_block % BF16_SUBLANES == 0 else F32


def _silu(x):
    return x * _sigmoid(x)


def _log_forget(f, lbp, layer):
    e = jnp.exp(lbp - jnp.max(lbp, axis=0, keepdims=True))
    p = e / jnp.sum(e, axis=0, keepdims=True)
    acc = p[0:1]
    for j in range(1, layer + 1):
        acc = acc + p[j:j + 1]
    lb = acc - p[0:1]
    ef = jnp.exp(-jnp.abs(f))
    r = 1.0 / (1.0 + ef)
    sig = jnp.where(f >= 0, 1.0, ef) * r
    ls = jnp.minimum(f, 0.0) - jnp.log(1.0 + ef)
    lf_pos = jnp.log(jnp.maximum(lb, TINY) + (1.0 - lb) * sig)
    return jnp.where(lb > 0, lf_pos, jnp.log1p(-lb) + ls)


def _norm_matmul_kernel(*refs, segments, layer):
    if any(mode == "logf" for _, _, mode in segments):
        x_ref, g_ref, w_ref, lb_ref, o_ref, xn_ref = refs
    else:
        x_ref, g_ref, w_ref, o_ref, xn_ref = refs
        lb_ref = None
    j = pl.program_id(1)

    @pl.when(j == 0)
    def _():
        x = x_ref[...]
        ms = jnp.mean(x * x, axis=-1, keepdims=True)
        xn_ref[...] = (x * lax.rsqrt(ms + EPS) * g_ref[...]).astype(BF16)

    for j0, j1, mode in segments:
        @pl.when((j >= j0) & (j < j1))
        def _(mode=mode):
            acc = jnp.dot(xn_ref[...], w_ref[...], preferred_element_type=F32)
            if mode == "silu":
                acc = _silu(acc)
            elif mode == "silu_dk":
                acc = _silu(acc) * (DK_A ** -0.5)
            elif mode == "scale_hd":
                acc = acc * (HD_B ** -0.5)
            elif mode == "logf":
                acc = _log_forget(acc, lb_ref[...], layer)
            else:
                assert mode == "id", mode
            o_ref[...] = acc.astype(o_ref.dtype)


def norm_matmul(x, g_all, w_all, layer, col0, segments, out_dtype, lb_a=None, tn=1024):
    n, d = x.shape
    nl = w_all.shape[0]
    tm = _row_tile(n, 1088)
    jb = col0 // tn
    bounds, j0 = [], 0
    for ncols_seg, mode in segments:
        bounds.append((j0, j0 + ncols_seg // tn, mode))
        j0 += ncols_seg // tn
    ncols = j0 * tn
    in_specs = [
        pl.BlockSpec((tm, d), lambda i, j: (i, 0)),
        pl.BlockSpec((None, 1, d), lambda i, j: (layer, 0, 0)),
        pl.BlockSpec((None, d, tn), lambda i, j: (layer, 0, jb + j)),
    ]
    args = [x, g_all.reshape(nl, 1, d), w_all]
    for ja, jz, mode in bounds:
        if mode == "logf":
            in_specs.append(pl.BlockSpec((lb_a.shape[0], tn),
                                         lambda i, j, ja=ja, jz=jz: (0, jnp.clip(j - ja, 0, jz - ja - 1))))
            args.append(lb_a)
    kern = functools.partial(_norm_matmul_kernel, segments=tuple(bounds), layer=layer)
    return pl.pallas_call(
        kern,
        grid=(n // tm, ncols // tn),
        in_specs=in_specs,
        out_specs=pl.BlockSpec((tm, tn), lambda i, j: (i, j)),
        out_shape=jax.ShapeDtypeStruct((n, ncols), out_dtype),
        scratch_shapes=[pltpu.VMEM((tm, d), BF16)],
        compiler_params=_params("arbitrary", "arbitrary"),
        name="norm_matmul",
    )(*args)


def _kv_proj_kernel(x_ref, g_ref, w_ref, o_ref):
    x = x_ref[...]
    ms = jnp.mean(x * x, axis=-1, keepdims=True)
    xn = (x * lax.rsqrt(ms + EPS) * g_ref[...]).astype(BF16)
    kv = jnp.dot(xn, w_ref[...], preferred_element_type=F32)
    ones = jnp.ones((kv.shape[0], HD_B), F32)
    for j in range(KV_B):
        k_j = kv[:, j * HD_B:(j + 1) * HD_B]
        o_ref[j] = jnp.concatenate([k_j, k_j], axis=1)
        v_j = kv[:, (KV_B + j) * HD_B:(KV_B + j + 1) * HD_B]
        o_ref[KV_B + j] = jnp.concatenate([v_j, ones], axis=1)


def kv_proj(x, g, w):
    n, d = x.shape
    tm = _row_tile(n, 544)
    return pl.pallas_call(
        _kv_proj_kernel,
        grid=(n // tm,),
        in_specs=[
            pl.BlockSpec((tm, d), lambda i: (i, 0)),
            pl.BlockSpec((1, d), lambda i: (0, 0)),
            pl.BlockSpec((d, 2 * KV_B * HD_B), lambda i: (0, 0)),
        ],
        out_specs=pl.BlockSpec((2 * KV_B, tm, 2 * HD_B), lambda i: (0, i, 0)),
        out_shape=jax.ShapeDtypeStruct((2 * KV_B, n, 2 * HD_B), F32),
        compiler_params=_params("arbitrary"),
        name="kv_proj",
    )(x, g.reshape(1, d), w)


def _matmul_res_kernel(y_ref, w_ref, x_ref, o_ref):
    o_ref[...] = x_ref[...] + jnp.dot(y_ref[...].astype(BF16), w_ref[...], preferred_element_type=F32)


def matmul_residual(y, w_all, layer, x):
    n, e = y.shape
    d = w_all.shape[2]
    tm = _row_tile(n, 544)
    return pl.pallas_call(
        _matmul_res_kernel,
        grid=(n // tm,),
        in_specs=[
            pl.BlockSpec((tm, e), lambda i: (i, 0)),
            pl.BlockSpec((None, e, d), lambda i: (layer, 0, 0), pipeline_mode=pl.Buffered(1)),
            pl.BlockSpec((tm, d), lambda i: (i, 0)),
        ],
        out_specs=pl.BlockSpec((tm, d), lambda i: (i, 0)),
        out_shape=jax.ShapeDtypeStruct((n, d), F32),
        compiler_params=_params("arbitrary"),
        name="matmul_residual",
    )(y, w_all, x)


def _rmsnorm_kernel(x_ref, g_ref, o_ref):
    x = x_ref[...]
    ms = jnp.mean(x * x, axis=-1, keepdims=True)
    o_ref[...] = x * lax.rsqrt(ms + EPS) * g_ref[...]


def final_norm(x, g, tm, n_out_blocks, in_block_of):
    d = x.shape[1]
    return pl.pallas_call(
        _rmsnorm_kernel,
        grid=(n_out_blocks,),
        in_specs=[
            pl.BlockSpec((tm, d), lambda t: (in_block_of(t), 0)),
            pl.BlockSpec((1, d), lambda t: (0, 0)),
        ],
        out_specs=pl.BlockSpec((tm, d), lambda t: (t, 0)),
        out_shape=jax.ShapeDtypeStruct((n_out_blocks * tm, d), F32),
        compiler_params=_params("arbitrary"),
        name="final_norm",
    )(x, g.reshape(1, d))


def _hgrn_kernel(*refs, chunk, n_chunks, hb, zero_init, has_prev):
    refs = list(refs)
    zq_ref, zf_ref, zi_ref, zg_ref, on_ref = refs[:5]
    pos = 5
    s0_ref = None
    if not zero_init:
        s0_ref = refs[pos]
        pos += 1
    if has_prev:
        pos += 1
    y_ref, sfin_ref, s_ref, cum_sc, oint_sc, flag_ref = refs[pos:]
    c = chunk
    t = pl.program_id(2)

    @pl.when(t == 0)
    def _():
        if zero_init:
            s_ref[...] = jnp.zeros_like(s_ref)
        else:
            s_ref[...] = s0_ref[0]

    on_g = on_ref[...]

    ri = lax.broadcasted_iota(jnp.int32, (c, c), 0)
    ci_ = lax.broadcasted_iota(jnp.int32, (c, c), 1)
    tril = (ci_ <= ri).astype(BF16)
    row_c = lax.broadcasted_iota(jnp.int32, (c, LANES), 0)
    col_c = lax.broadcasted_iota(jnp.int32, (c, LANES), 1)
    causal = col_c <= row_c
    zpad_k = jnp.zeros((LANES - c, DK_A), F32)
    zpad_v = jnp.zeros((LANES - c, DV_A), BF16)
    zpad_x = jnp.zeros((LANES - c - 8, DK_A), F32)

    def emit_y(o, rows, vs_):
        ms = jnp.mean(o * o, axis=-1, keepdims=True)
        o = o * lax.rsqrt(ms + EPS) * on_g
        y_ref[rows, vs_] = (o * zg_ref[rows, vs_].astype(F32)).astype(y_ref.dtype)

    for ci in range(n_chunks):
        rows = slice(ci * c, (ci + 1) * c)
        logf = zf_ref[rows, :]
        h1 = logf.astype(BF16)
        h2 = (logf - h1.astype(F32)).astype(BF16)
        cum = jnp.dot(tril, h1, preferred_element_type=F32) + jnp.dot(tril, h2, preferred_element_type=F32)
        last = cum[c - 1:c, :]
        mid = cum[c // 2 - 1:c // 2, :]
        dev = jnp.max(jnp.maximum(cum[0:1, :] - mid, mid - last))
        flag_ref[ci] = (dev <= FACTOR_SAFE).astype(jnp.int32)
        cum_sc[rows, :] = cum

        for h in range(hb):
            hs = slice(h * DK_A, (h + 1) * DK_A)
            vs_ = slice(h * DV_A, (h + 1) * DV_A)
            q_h = zq_ref[rows, hs]
            k_h = 1.0 - jnp.exp(zf_ref[rows, hs])
            cum_h = cum_sc[rows, hs]
            v_pad = jnp.concatenate([zi_ref[rows, vs_].astype(BF16), zpad_v], axis=0)
            d_h = cum_h - mid[:, hs]
            qm = (q_h * jnp.exp(d_h)).astype(BF16)
            km = jnp.concatenate([k_h * jnp.exp(-d_h), zpad_k], axis=0).astype(BF16)
            att = lax.dot_general(qm, km, (((1,), (1,)), ((), ())), preferred_element_type=F32)
            att = jnp.where(causal, att, 0.0).astype(BF16)
            o_intra = jnp.dot(att, v_pad, preferred_element_type=F32)
            s_prev = s_ref[h]
            qa = (q_h * jnp.exp(cum_h)).astype(BF16)
            o_inter = jnp.dot(qa, s_prev.astype(BF16), preferred_element_type=F32)
            oint_sc[rows, vs_] = o_inter
            kd = k_h * jnp.exp(last[:, hs] - cum_h)
            el8 = jnp.broadcast_to(jnp.exp(last[:, hs]), (8, DK_A))
            x_t = jnp.concatenate([kd, el8, zpad_x], axis=0).T
            s_ref[h] = x_t[:, c:c + 1] * s_prev + jnp.dot(x_t.astype(BF16), v_pad, preferred_element_type=F32)
            emit_y(o_inter + o_intra, rows, vs_)

    for ci in range(n_chunks):
        rows = slice(ci * c, (ci + 1) * c)

        @pl.when(flag_ref[ci] == 0)
        def _(rows=rows):
            row_v = lax.broadcasted_iota(jnp.int32, (c, 1), 0)

            def row_of(a, s):
                return jnp.sum(jnp.where(row_v == s, a, 0.0), axis=0, keepdims=True)

            for h in range(hb):
                hs = slice(h * DK_A, (h + 1) * DK_A)
                vs_ = slice(h * DV_A, (h + 1) * DV_A)
                q_h = zq_ref[rows, hs]
                k_h = 1.0 - jnp.exp(zf_ref[rows, hs])
                cum_h = cum_sc[rows, hs]
                v_h = zi_ref[rows, vs_].astype(F32)

                def body(s, acc_o, q_h=q_h, k_h=k_h, cum_h=cum_h, v_h=v_h):
                    dec = jnp.exp(jnp.minimum(cum_h - row_of(cum_h, s), 0.0))
                    w = jnp.sum(q_h * row_of(k_h, s) * dec, axis=-1, keepdims=True)
                    w = jnp.where(row_v >= s, w, 0.0)
                    return acc_o + w * row_of(v_h, s)

                o_intra = lax.fori_loop(0, c, body, jnp.zeros((c, DV_A), F32))
                emit_y(oint_sc[rows, vs_] + o_intra, rows, vs_)

    @pl.when(t == pl.num_programs(2) - 1)
    def _():
        sfin_ref[0] = s_ref[...]


def hgrn_scan(zqf, zig, onorm_a, s0_all, states_prev, *, layer, n_batch, rows_per_batch, chunk, n_chunks, hb):
    n = zqf.shape[0]
    tb = chunk * n_chunks
    nt = rows_per_batch // tb
    ng = H_A // hb
    wk, wv = hb * DK_A, hb * DV_A
    f_off = (H_A * DK_A) // wk
    g_off = E_WIDTH // wv
    row = lambda b, g, t: b * nt + t
    in_specs = [
        pl.BlockSpec((tb, wk), lambda b, g, t: (row(b, g, t), g)),
        pl.BlockSpec((tb, wk), lambda b, g, t: (row(b, g, t), f_off + g)),
        pl.BlockSpec((tb, wv), lambda b, g, t: (row(b, g, t), g)),
        pl.BlockSpec((tb, wv), lambda b, g, t: (row(b, g, t), g_off + g)),
        pl.BlockSpec((None, 1, DV_A), lambda b, g, t: (layer, 0, 0)),
    ]
    args = [zqf, zqf, zig, zig, onorm_a.reshape(N_A, 1, DV_A)]
    state_spec = pl.BlockSpec((None, 1, hb, DK_A, DV_A), lambda b, g, t: (layer, b, g, 0, 0))
    if s0_all is not None:
        in_specs.append(state_spec)
        args.append(s0_all)
    aliases = {}
    if states_prev is not None:
        aliases = {len(args): 1}
        in_specs.append(pl.BlockSpec(memory_space=pl.ANY))
        args.append(states_prev)
    kern = functools.partial(_hgrn_kernel, chunk=chunk, n_chunks=n_chunks, hb=hb,
                             zero_init=s0_all is None, has_prev=states_prev is not None)
    return pl.pallas_call(
        kern,
        grid=(n_batch, ng, nt),
        in_specs=in_specs,
        out_specs=[
            pl.BlockSpec((tb, wv), lambda b, g, t: (row(b, g, t), g)),
            state_spec,
        ],
        out_shape=[
            jax.ShapeDtypeStruct((n, E_WIDTH), _act_dtype(tb)),
            jax.ShapeDtypeStruct((N_A, n_batch, H_A, DK_A, DV_A), F32),
        ],
        scratch_shapes=[
            pltpu.VMEM((hb, DK_A, DV_A), F32),
            pltpu.VMEM((tb, wk), F32),
            pltpu.VMEM((tb, wv), F32),
            pltpu.SMEM((n_chunks,), jnp.int32),
        ],
        input_output_aliases=aliases,
        compiler_params=_params("arbitrary", "arbitrary", "arbitrary"),
        name=f"hgrn_scan_c{chunk}",
    )(*args)


def _swa_prompt_kernel(sink_ref, q_ref, g_ref, kc_ref, kp_ref, vc_ref, vp_ref, y_ref, d_sc, s_sc, p_sc):
    blk = pl.program_id(1)
    kg = pl.program_id(2)
    w = WINDOW
    npair = G_B // 2

    @pl.when(kg == 0)
    def _():
        row = lax.broadcasted_iota(jnp.int32, (w, 2 * w), 0)
        col = lax.broadcasted_iota(jnp.int32, (w, 2 * w), 1)
        dist = w + row - col
        key_pos = (blk - 1) * w + col
        valid = (dist >= 0) & (dist < w) & (key_pos >= PAD)
        d_sc[...] = jnp.where(valid, dist.astype(F32), -NEG_BIG)

    lane = lax.broadcasted_iota(jnp.int32, (w, LANES), 1)
    lane_k = lax.broadcasted_iota(jnp.int32, (2 * w, LANES), 1)
    row_k = lax.broadcasted_iota(jnp.int32, (2 * w, LANES), 0)
    for kk in range(KH_STEP):
        kh = kg * KH_STEP + kk
        kdup = jnp.concatenate([kp_ref[kk], kc_ref[kk]], axis=0)
        k_par = [jnp.where(lane_k < HD_B, kdup, 0.0).astype(BF16), jnp.where(lane_k < HD_B, 0.0, kdup).astype(BF16)]
        vraw = jnp.concatenate([vp_ref[kk], vc_ref[kk]], axis=0)
        vext = jnp.where(row_k == 0, jnp.where(lane_k < HD_B, 0.0, vraw), vraw).astype(BF16)
        qs = jnp.concatenate(
            [q_ref[:, (kk * npair + j) * LANES:(kk * npair + j + 1) * LANES] for j in range(npair)],
            axis=0).astype(BF16)
        for par in range(2):
            s_sc[kk, par] = lax.dot_general(qs, k_par[par], (((1,), (1,)), ((), ())),
                                            preferred_element_type=F32)
        for j in range(npair):
            rows = slice(j * w, (j + 1) * w)
            for par in range(2):
                head = kh * G_B + 2 * j + par
                slope = jnp.exp(-LN2 * 8.0 * (head + 1).astype(F32) / H_B)
                sink = sink_ref[head]
                s_lo = s_sc[kk, par, rows, :w] - slope * d_sc[:, :w]
                s_lo = jnp.where(lane == 0, sink, s_lo)
                s_hi = s_sc[kk, par, rows, w:] - slope * d_sc[:, w:]
                m = jnp.max(jnp.maximum(s_lo, s_hi), axis=-1, keepdims=True)
                p_sc[kk, par, rows, :w] = jnp.exp(s_lo - m).astype(BF16)
                p_sc[kk, par, rows, w:] = jnp.exp(s_hi - m).astype(BF16)
        o_par = [jnp.dot(p_sc[kk, par], vext, preferred_element_type=F32) for par in range(2)]
        for j in range(npair):
            o_even = o_par[0][j * w:(j + 1) * w, :]
            o_odd = o_par[1][j * w:(j + 1) * w, :]
            lo = o_even / pltpu.roll(o_even, HD_B, 1)
            hi = pltpu.roll(o_odd, HD_B, 1) / o_odd
            cs = slice((kk * npair + j) * LANES, (kk * npair + j + 1) * LANES)
            y_ref[:, cs] = (jnp.where(lane < HD_B, lo, hi) * g_ref[:, cs].astype(F32)).astype(BF16)


def swa_prompt(zb, kv_hm, sink, n_batch, blocks_per_batch):
    n = zb.shape[0]
    w = WINDOW
    gw = KH_STEP * G_B * HD_B
    nb = blocks_per_batch
    nkg = KV_B // KH_STEP
    cur = lambda b, i, kg: b * nb + i
    prev = lambda b, i, kg: b * nb + jnp.maximum(i - 1, 0)
    kv_block = (KH_STEP, w, 2 * HD_B)
    return pl.pallas_call(
        _swa_prompt_kernel,
        grid=(n_batch, nb, nkg),
        in_specs=[
            pl.BlockSpec(memory_space=pltpu.SMEM),
            pl.BlockSpec((w, gw), lambda b, i, kg: (cur(b, i, kg), kg)),
            pl.BlockSpec((w, gw), lambda b, i, kg: (cur(b, i, kg), nkg + kg)),
            pl.BlockSpec(kv_block, lambda b, i, kg: (kg, cur(b, i, kg), 0)),
            pl.BlockSpec(kv_block, lambda b, i, kg: (kg, prev(b, i, kg), 0)),
            pl.BlockSpec(kv_block, lambda b, i, kg: (nkg + kg, cur(b, i, kg), 0)),
            pl.BlockSpec(kv_block, lambda b, i, kg: (nkg + kg, prev(b, i, kg), 0)),
        ],
        out_specs=pl.BlockSpec((w, gw), lambda b, i, kg: (cur(b, i, kg), kg)),
        out_shape=jax.ShapeDtypeStruct((n, E_WIDTH), BF16),
        scratch_shapes=[
            pltpu.VMEM((w, 2 * w), F32),
            pltpu.VMEM((KH_STEP, 2, G_B // 2 * w, 2 * w), F32),
            pltpu.VMEM((KH_STEP, 2, G_B // 2 * w, 2 * w), BF16),
        ],
        compiler_params=_params("arbitrary", "arbitrary", "arbitrary"),
        name="swa_prompt",
    )(sink, zb, zb, kv_hm, kv_hm, kv_hm, kv_hm)


def _swa_sample_kernel(sink_ref, q_ref, g_ref, kn_ref, vn_ref, ck_ref, cv_ref, y_ref, *, t_new):
    wb = ck_ref.shape[1]
    nk = wb + t_new
    nr = G_B * t_new
    row = lax.broadcasted_iota(jnp.int32, (nr, nk), 0)
    col = lax.broadcasted_iota(jnp.int32, (nr, nk), 1)
    tok = row % t_new
    grp = row // t_new
    dist = wb + tok - col
    valid = (dist >= 0) & (dist < WINDOW)
    distf = dist.astype(F32)
    grp1 = lax.broadcasted_iota(jnp.int32, (nr, 1), 0) // t_new
    for kh in range(KV_B):
        ks = slice(kh * HD_B, (kh + 1) * HD_B)
        kf = jnp.concatenate([ck_ref[0, :, ks], kn_ref[kh][:, :HD_B]], axis=0).astype(BF16)
        vf = jnp.concatenate([cv_ref[0, :, ks], vn_ref[kh][:, :HD_B]], axis=0).astype(BF16)
        qs = jnp.concatenate(
            [q_ref[:, (kh * G_B + g) * HD_B:(kh * G_B + g + 1) * HD_B] for g in range(G_B)], axis=0)
        s = lax.dot_general(qs.astype(BF16), kf, (((1,), (1,)), ((), ())), preferred_element_type=F32)
        slope = jnp.exp(-LN2 * 8.0 * (kh * G_B + grp + 1).astype(F32) / H_B)
        sink = jnp.zeros((nr, 1), F32)
        for g in range(G_B):
            sink = jnp.where(grp1 == g, sink_ref[kh * G_B + g], sink)
        s = s - slope * distf
        s = jnp.where(valid, s, NEG_BIG)
        m = jnp.maximum(jnp.max(s, axis=-1, keepdims=True), sink)
        p = jnp.where(valid, jnp.exp(s - m), 0.0)
        den = jnp.sum(p, axis=-1, keepdims=True) + jnp.exp(sink - m)
        o = jnp.dot(p.astype(BF16), vf, preferred_element_type=F32) / den
        for g in range(G_B):
            hsl = slice((kh * G_B + g) * HD_B, (kh * G_B + g + 1) * HD_B)
            y_ref[:, hsl] = (o[g * t_new:(g + 1) * t_new, :] * g_ref[:, hsl].astype(F32)).astype(y_ref.dtype)


def swa_sample(zb, kv_hm, sink, cache_k, cache_v, n_batch, t_new):
    n = zb.shape[0]
    wb = cache_k.shape[1]
    kern = functools.partial(_swa_sample_kernel, t_new=t_new)
    return pl.pallas_call(
        kern,
        grid=(n_batch,),
        in_specs=[
            pl.BlockSpec(memory_space=pltpu.SMEM),
            pl.BlockSpec((t_new, E_WIDTH), lambda b: (b, 0)),
            pl.BlockSpec((t_new, E_WIDTH), lambda b: (b, 1)),
            pl.BlockSpec((KV_B, t_new, 2 * HD_B), lambda b: (0, b, 0)),
            pl.BlockSpec((KV_B, t_new, 2 * HD_B), lambda b: (1, b, 0)),
            pl.BlockSpec((1, wb, KV_B * HD_B), lambda b: (b, 0, 0)),
            pl.BlockSpec((1, wb, KV_B * HD_B), lambda b: (b, 0, 0)),
        ],
        out_specs=pl.BlockSpec((t_new, E_WIDTH), lambda b: (b, 0)),
        out_shape=jax.ShapeDtypeStruct((n, E_WIDTH), _act_dtype(t_new)),
        compiler_params=_params("arbitrary"),
        name="swa_sample",
    )(sink, zb, zb, kv_hm, kv_hm, cache_k, cache_v)


def _trunk(x, s0_all, attend, weights, *, n_batch, rows_per_batch, chunk, n_chunks, hb):
    (norm_a, w_in_a, lb_a, onorm_a, w_out_a, norm_kv, w_kv, norm_b, w_in_b, sink_b, w_out_b) = weights
    fd = 2 * H_A * DK_A
    act = _act_dtype(chunk * n_chunks)
    states = None
    for layer in range(N_A):
        zqf = norm_matmul(x, norm_a, w_in_a, layer, 0, ((fd // 2, "silu_dk"), (fd // 2, "logf")), F32, lb_a=lb_a)
        zig = norm_matmul(x, norm_a, w_in_a, layer, fd, ((E_WIDTH, "id"), (E_WIDTH, "silu")), act)
        y, states = hgrn_scan(zqf, zig, onorm_a, s0_all, states, layer=layer, n_batch=n_batch,
                              rows_per_batch=rows_per_batch, chunk=chunk, n_chunks=n_chunks, hb=hb)
        x = matmul_residual(y, w_out_a, layer, x)
    kv_hm = kv_proj(x, norm_kv, w_kv)
    for layer in range(N_B):
        zb = norm_matmul(x, norm_b, w_in_b, layer, 0, ((E_WIDTH, "scale_hd"), (E_WIDTH, "silu")), act)
        y = attend(zb, kv_hm, sink_b[layer])
        x = matmul_residual(y, w_out_b, layer, x)
    return x, states, kv_hm


def kernel(x_prompt, x_sample, state_hgrn, cache_k, cache_v, meta_tokens, norm_a, w_in_a, lb_a, onorm_a,
           w_out_a, norm_kv, w_kv, norm_b, w_in_b, sink_b, w_out_b, norm_f):
    bsz, seq, d = x_prompt.shape
    dec_b, dec_t, _ = x_sample.shape
    wb = cache_k.shape[1]
    weights = (norm_a, w_in_a.astype(BF16), lb_a, onorm_a, w_out_a.astype(BF16), norm_kv, w_kv.astype(BF16),
               norm_b, w_in_b.astype(BF16), sink_b, w_out_b.astype(BF16))

    rows_p = PAD + N_META + seq
    nb_p = rows_p // ROW_BLOCK
    xp = jnp.concatenate([
        jnp.zeros((bsz, PAD, d), x_prompt.dtype),
        jnp.broadcast_to(meta_tokens.astype(x_prompt.dtype)[None], (bsz, N_META, d)),
        x_prompt], axis=1).reshape(bsz * rows_p, d)
    attend_p = functools.partial(swa_prompt, n_batch=bsz, blocks_per_batch=nb_p)
    xp, sp, kvp = _trunk(xp, None, attend_p, weights, n_batch=bsz, rows_per_batch=rows_p, chunk=64,
                         n_chunks=ROW_BLOCK // 64, hb=8)
    blocks_out = seq // ROW_BLOCK
    y_prompt = final_norm(xp, norm_f, ROW_BLOCK, bsz * blocks_out,
                          lambda t: (t // blocks_out) * nb_p + 1 + t % blocks_out).reshape(bsz, seq, d)
    kvp = kvp.reshape(2, KV_B, bsz, rows_p, 2 * HD_B)[:, :, :, rows_p - wb:, :HD_B]
    cache_k_prompt = jnp.transpose(kvp[0], (1, 2, 0, 3)).astype(cache_k.dtype)
    cache_v_prompt = jnp.transpose(kvp[1], (1, 2, 0, 3)).astype(cache_v.dtype)

    xs = x_sample.reshape(dec_b * dec_t, d)
    attend_s = functools.partial(swa_sample, cache_k=cache_k.reshape(dec_b, wb, KV_B * HD_B),
                                 cache_v=cache_v.reshape(dec_b, wb, KV_B * HD_B), n_batch=dec_b, t_new=dec_t)
    xs, ss, kvs = _trunk(xs, state_hgrn.astype(F32), attend_s, weights, n_batch=dec_b, rows_per_batch=dec_t,
                         chunk=dec_t, n_chunks=1, hb=8)
    y_sample = final_norm(xs, norm_f, dec_b * dec_t, 1, lambda t: t).reshape(dec_b, dec_t, d)
    kvs = kvs.reshape(2, KV_B, dec_b, dec_t, 2 * HD_B)[..., :HD_B]
    k_new = jnp.transpose(kvs[0], (1, 2, 0, 3)).astype(cache_k.dtype)
    v_new = jnp.transpose(kvs[1], (1, 2, 0, 3)).astype(cache_v.dtype)
    cache_k_sample = jnp.concatenate([cache_k, k_new], axis=1)[:, -wb:]
    cache_v_sample = jnp.concatenate([cache_v, v_new], axis=1)[:, -wb:]

    return (y_prompt, y_sample, sp.astype(state_hgrn.dtype), cache_k_prompt, cache_v_prompt,
            ss.astype(state_hgrn.dtype), cache_k_sample, cache_v_sample)
```

```python
import functools

import jax
import jax.numpy as jnp
from jax import lax
from jax.experimental import pallas as pl
from jax.experimental.pallas import tpu as pltpu

D_MODEL = 2048
E_WIDTH = 2 * D_MODEL
N_A = 2
N_B = 2
DK_A = 128
H_A = D_MODEL // DK_A
DV_A = E_WIDTH // H_A
HD_B = 64
H_B = E_WIDTH // HD_B
KV_B = H_B // 8
G_B = H_B // KV_B
WINDOW = 128
N_META = 16
EPS = 1e-6
NEG_BIG = -1e30
TINY = 1e-30

LANES = 128
BF16_SUBLANES = 16
ROW_BLOCK = 128
PAD = ROW_BLOCK - N_META
FACTOR_SAFE = 60.0
LN2 = 0.6931471805599453
VMEM_LIMIT = 56 * 1024 * 1024
KH_STEP = 8
SAMPLE_BATCHES = BF16_SUBLANES // 8
PROJ_ROWS = 1120
RES_ROWS = 448
PROJ_COLS = 1024

F32 = jnp.float32
BF16 = jnp.bfloat16


def _sigmoid(x):
    return 1.0 / (1.0 + jnp.exp(-x))


def _silu(x):
    return x * _sigmoid(x)


def _params(*sem):
    return pltpu.CompilerParams(dimension_semantics=sem, vmem_limit_bytes=VMEM_LIMIT)


def _row_tile(n, cap):
    for t in range(min(cap, n) // BF16_SUBLANES * BF16_SUBLANES, 0, -BF16_SUBLANES):
        if n % t == 0:
            return t
    raise ValueError(f"no row tile for {n}")


def _prenorm_kernel(x_ref, g_ref, o_ref):
    x = x_ref[...]
    ms = jnp.mean(x * x, axis=-1, keepdims=True)
    o_ref[...] = (x * lax.rsqrt(ms + EPS) * g_ref[...]).astype(o_ref.dtype)


def prenorm(x, g_all, layer):
    n, d = x.shape
    tm = _row_tile(n, RES_ROWS)
    return pl.pallas_call(
        _prenorm_kernel,
        grid=(n // tm,),
        in_specs=[
            pl.BlockSpec((tm, d), lambda i: (i, 0)),
            pl.BlockSpec((None, 1, d), lambda i: (layer, 0, 0)),
        ],
        out_specs=pl.BlockSpec((tm, d), lambda i: (i, 0)),
        out_shape=jax.ShapeDtypeStruct((n, d), BF16),
        compiler_params=_params("arbitrary"),
        name="prenorm",
    )(x, g_all.reshape(g_all.shape[0], 1, d))


def _log_forget(f, lbp, layer):
    e = jnp.exp(lbp - jnp.max(lbp, axis=0, keepdims=True))
    p = e / jnp.sum(e, axis=0, keepdims=True)
    acc = p[0:1]
    for j in range(1, layer + 1):
        acc = acc + p[j:j + 1]
    lb = acc - p[0:1]
    ef = jnp.exp(-jnp.abs(f))
    r = 1.0 / (1.0 + ef)
    sig = jnp.where(f >= 0, 1.0, ef) * r
    ls = jnp.minimum(f, 0.0) - jnp.log(1.0 + ef)
    lf_pos = jnp.log(jnp.maximum(lb, TINY) + (1.0 - lb) * sig)
    return jnp.where(lb > 0, lf_pos, jnp.log1p(-lb) + ls)


def _proj_kernel(*refs, segments, layer):
    if any(mode == "logf" for _, _, mode in segments):
        x_ref, w_ref, lb_ref, o_ref, wb_ref = refs
    else:
        x_ref, w_ref, o_ref, wb_ref = refs
        lb_ref = None
    j = pl.program_id(0)

    @pl.when(pl.program_id(1) == 0)
    def _():
        wb_ref[...] = w_ref[...].astype(BF16)

    for j0, j1, mode in segments:
        @pl.when((j >= j0) & (j < j1))
        def _(mode=mode):
            acc = jnp.dot(x_ref[...], wb_ref[...], preferred_element_type=F32)
            if mode == "silu":
                acc = _silu(acc)
            elif mode == "silu_dk":
                acc = _silu(acc) * (DK_A ** -0.5)
            elif mode == "scale_hd":
                acc = acc * (HD_B ** -0.5)
            elif mode == "logf":
                acc = _log_forget(acc, lb_ref[...], layer)
            else:
                assert mode == "id", mode
            o_ref[...] = acc.astype(o_ref.dtype)


def proj(xn, w_all, layer, col0, segments, out_dtype, lb_a=None):
    n, d = xn.shape
    tn = PROJ_COLS
    tm = _row_tile(n, PROJ_ROWS)
    jb = col0 // tn
    bounds, j0 = [], 0
    for ncols_seg, mode in segments:
        bounds.append((j0, j0 + ncols_seg // tn, mode))
        j0 += ncols_seg // tn
    ncols = j0 * tn
    in_specs = [
        pl.BlockSpec((tm, d), lambda j, i: (i, 0)),
        pl.BlockSpec((None, d, tn), lambda j, i: (layer, 0, jb + j)),
    ]
    args = [xn, w_all]
    for ja, jz, mode in bounds:
        if mode == "logf":
            in_specs.append(pl.BlockSpec((lb_a.shape[0], tn),
                                         lambda j, i, ja=ja, jz=jz: (0, jnp.clip(j - ja, 0, jz - ja - 1))))
            args.append(lb_a)
    kern = functools.partial(_proj_kernel, segments=tuple(bounds), layer=layer)
    return pl.pallas_call(
        kern,
        grid=(ncols // tn, n // tm),
        in_specs=in_specs,
        out_specs=pl.BlockSpec((tm, tn), lambda j, i: (i, j)),
        out_shape=jax.ShapeDtypeStruct((n, ncols), out_dtype),
        scratch_shapes=[pltpu.VMEM((d, tn), BF16)],
        compiler_params=_params("arbitrary", "arbitrary"),
        name="proj",
    )(*args)


def _kv_proj_kernel(x_ref, w_ref, o_ref, wb_ref):
    @pl.when(pl.program_id(0) == 0)
    def _():
        wb_ref[...] = w_ref[...].astype(BF16)

    kv = jnp.dot(x_ref[...], wb_ref[...], preferred_element_type=F32)
    ones = jnp.ones((kv.shape[0], HD_B), F32)
    for j in range(KV_B):
        k_j = kv[:, j * HD_B:(j + 1) * HD_B]
        o_ref[j] = jnp.concatenate([k_j, k_j], axis=1)
        v_j = kv[:, (KV_B + j) * HD_B:(KV_B + j + 1) * HD_B]
        o_ref[KV_B + j] = jnp.concatenate([v_j, ones], axis=1)


def kv_proj(xn, w):
    n, d = xn.shape
    tm = _row_tile(n, RES_ROWS)
    return pl.pallas_call(
        _kv_proj_kernel,
        grid=(n // tm,),
        in_specs=[
            pl.BlockSpec((tm, d), lambda i: (i, 0)),
            pl.BlockSpec((d, 2 * KV_B * HD_B), lambda i: (0, 0), pipeline_mode=pl.Buffered(1)),
        ],
        out_specs=pl.BlockSpec((2 * KV_B, tm, 2 * HD_B), lambda i: (0, i, 0)),
        out_shape=jax.ShapeDtypeStruct((2 * KV_B, n, 2 * HD_B), F32),
        scratch_shapes=[pltpu.VMEM((d, 2 * KV_B * HD_B), BF16)],
        compiler_params=_params("arbitrary"),
        name="kv_proj",
    )(xn, w)


def _matmul_res_kernel(y_ref, w_ref, x_ref, g_ref, *out_refs, n_norm, keep_x):
    x = x_ref[...] + jnp.dot(y_ref[...], w_ref[...], preferred_element_type=F32)
    pos = 0
    if keep_x:
        out_refs[0][...] = x
        pos = 1
    ms = jnp.mean(x * x, axis=-1, keepdims=True)
    xh = x * lax.rsqrt(ms + EPS)
    for t in range(n_norm):
        out_refs[pos + t][...] = (xh * g_ref[t]).astype(out_refs[pos + t].dtype)


def matmul_residual(y, w_all, layer, x, gains, norm_dtype, keep_x=True):
    n, e = y.shape
    d = w_all.shape[2]
    tm = _row_tile(n, RES_ROWS)
    gs = jnp.stack([g.reshape(1, d) for g in gains])
    row_spec = pl.BlockSpec((tm, d), lambda i: (i, 0))
    out_shapes = ([jax.ShapeDtypeStruct((n, d), F32)] if keep_x else []) + \
        [jax.ShapeDtypeStruct((n, d), norm_dtype) for _ in gains]
    kern = functools.partial(_matmul_res_kernel, n_norm=len(gains), keep_x=keep_x)
    return pl.pallas_call(
        kern,
        grid=(n // tm,),
        in_specs=[
            pl.BlockSpec((tm, e), lambda i: (i, 0)),
            pl.BlockSpec((None, e, d), lambda i: (layer, 0, 0), pipeline_mode=pl.Buffered(1)),
            row_spec,
            pl.BlockSpec((len(gains), 1, d), lambda i: (0, 0, 0)),
        ],
        out_specs=[row_spec] * len(out_shapes),
        out_shape=out_shapes,
        compiler_params=_params("arbitrary"),
        name="matmul_residual",
    )(y, w_all, x, gs)


def _hgrn_kernel(*refs, chunk, n_chunks, hb, per_batch, zero_init, n_alias):
    refs = list(refs)
    zq_ref, zf_ref, zi_ref, zg_ref, on_ref = refs[:5]
    pos = 5
    s0_ref = None
    if not zero_init:
        s0_ref = refs[pos]
        pos += 1
    pos += n_alias
    y_ref, sfin_ref, s_ref, cum_sc, oint_sc, ystage_sc, flag_ref = refs[pos:]
    c = chunk
    t = pl.program_id(2)
    aligned = c % BF16_SUBLANES == 0

    @pl.when(t == 0)
    def _():
        if zero_init:
            s_ref[...] = jnp.zeros_like(s_ref)
        else:
            s_ref[...] = s0_ref[...]

    on_g = on_ref[...]
    if aligned:
        zi_all = zg_all = None
    else:
        zi_all = zi_ref[...].astype(F32)
        zg_all = zg_ref[...].astype(F32)

    ri = lax.broadcasted_iota(jnp.int32, (c, c), 0)
    ci_ = lax.broadcasted_iota(jnp.int32, (c, c), 1)
    tril = (ci_ <= ri).astype(BF16)
    row_c = lax.broadcasted_iota(jnp.int32, (c, LANES), 0)
    col_c = lax.broadcasted_iota(jnp.int32, (c, LANES), 1)
    causal = col_c <= row_c
    zpad_k = jnp.zeros((LANES - c, DK_A), F32)
    zpad_x = jnp.zeros((LANES - c - 8, DK_A), F32)

    def values(rows, vs_):
        return zi_ref[rows, vs_].astype(F32) if aligned else zi_all[rows, vs_]

    def values_padded(rows, vs_):
        if aligned:
            return jnp.concatenate([zi_ref[rows, vs_], jnp.zeros((LANES - c, DV_A), BF16)], axis=0)
        return jnp.concatenate([zi_all[rows, vs_], jnp.zeros((LANES - c, DV_A), F32)], axis=0).astype(BF16)

    def emit_y(o, rows, vs_):
        ms = jnp.mean(o * o, axis=-1, keepdims=True)
        o = o * lax.rsqrt(ms + EPS) * on_g
        if aligned:
            y_ref[rows, vs_] = (o * zg_ref[rows, vs_].astype(F32)).astype(y_ref.dtype)
        else:
            ystage_sc[rows, vs_] = o * zg_all[rows, vs_]

    for ci in range(n_chunks):
        rows = slice(ci * c, (ci + 1) * c)
        si = ci if per_batch else 0
        logf = zf_ref[rows, :]
        h1 = logf.astype(BF16)
        h2 = (logf - h1.astype(F32)).astype(BF16)
        cum = jnp.dot(tril, h1, preferred_element_type=F32) + jnp.dot(tril, h2, preferred_element_type=F32)
        last = cum[c - 1:c, :]
        mid = cum[c // 2 - 1:c // 2, :]
        dev = jnp.max(jnp.maximum(cum[0:1, :] - mid, mid - last))
        flag_ref[ci] = (dev <= FACTOR_SAFE).astype(jnp.int32)
        cum_sc[rows, :] = cum

        for h in range(hb):
            hs = slice(h * DK_A, (h + 1) * DK_A)
            vs_ = slice(h * DV_A, (h + 1) * DV_A)
            q_h = zq_ref[rows, hs]
            k_h = 1.0 - jnp.exp(zf_ref[rows, hs])
            cum_h = cum_sc[rows, hs]
            v_pad = values_padded(rows, vs_)
            d_h = cum_h - mid[:, hs]
            qm = (q_h * jnp.exp(d_h)).astype(BF16)
            km = jnp.concatenate([k_h * jnp.exp(-d_h), zpad_k], axis=0).astype(BF16)
            att = lax.dot_general(qm, km, (((1,), (1,)), ((), ())), preferred_element_type=F32)
            att = jnp.where(causal, att, 0.0).astype(BF16)
            o_intra = jnp.dot(att, v_pad, preferred_element_type=F32)
            s_prev = s_ref[si, h]
            qa = (q_h * jnp.exp(cum_h)).astype(BF16)
            o_inter = jnp.dot(qa, s_prev.astype(BF16), preferred_element_type=F32)
            oint_sc[rows, vs_] = o_inter
            kd = k_h * jnp.exp(last[:, hs] - cum_h)
            el8 = jnp.broadcast_to(jnp.exp(last[:, hs]), (8, DK_A))
            x_t = jnp.concatenate([kd, el8, zpad_x], axis=0).T
            s_ref[si, h] = x_t[:, c:c + 1] * s_prev + jnp.dot(x_t.astype(BF16), v_pad, preferred_element_type=F32)
            emit_y(o_inter + o_intra, rows, vs_)

    for ci in range(n_chunks):
        rows = slice(ci * c, (ci + 1) * c)

        @pl.when(flag_ref[ci] == 0)
        def _(rows=rows):
            row_v = lax.broadcasted_iota(jnp.int32, (c, 1), 0)

            def row_of(a, s):
                return jnp.sum(jnp.where(row_v == s, a, 0.0), axis=0, keepdims=True)

            for h in range(hb):
                hs = slice(h * DK_A, (h + 1) * DK_A)
                vs_ = slice(h * DV_A, (h + 1) * DV_A)
                q_h = zq_ref[rows, hs]
                k_h = 1.0 - jnp.exp(zf_ref[rows, hs])
                cum_h = cum_sc[rows, hs]
                v_h = values(rows, vs_)

                def body(s, acc_o, q_h=q_h, k_h=k_h, cum_h=cum_h, v_h=v_h):
                    dec = jnp.exp(jnp.minimum(cum_h - row_of(cum_h, s), 0.0))
                    w = jnp.sum(q_h * row_of(k_h, s) * dec, axis=-1, keepdims=True)
                    w = jnp.where(row_v >= s, w, 0.0)
                    return acc_o + w * row_of(v_h, s)

                o_intra = lax.fori_loop(0, c, body, jnp.zeros((c, DV_A), F32))
                emit_y(oint_sc[rows, vs_] + o_intra, rows, vs_)

    if not aligned:
        y_ref[...] = ystage_sc[...].astype(y_ref.dtype)

    @pl.when(t == pl.num_programs(2) - 1)
    def _():
        sfin_ref[...] = s_ref[...]


def hgrn_scan(zqf, zig, onorm_a, s0_all, y_prev, states_prev, *, layer, n_batch, rows_per_batch, row0, chunk,
              n_chunks, hb, per_batch):
    n = zqf.shape[0]
    tb = chunk * n_chunks
    ns = n_chunks if per_batch else 1
    nt = 1 if per_batch else rows_per_batch // tb
    nbg = n_batch // ns
    ng = H_A // hb
    wk, wv = hb * DK_A, hb * DV_A
    f_off = (H_A * DK_A) // wk
    g_off = E_WIDTH // wv
    rb0 = row0 // tb
    row = lambda b, g, t: rb0 + b * nt + t
    in_specs = [
        pl.BlockSpec((tb, wk), lambda b, g, t: (row(b, g, t), g)),
        pl.BlockSpec((tb, wk), lambda b, g, t: (row(b, g, t), f_off + g)),
        pl.BlockSpec((tb, wv), lambda b, g, t: (row(b, g, t), g)),
        pl.BlockSpec((tb, wv), lambda b, g, t: (row(b, g, t), g_off + g)),
        pl.BlockSpec((None, 1, DV_A), lambda b, g, t: (layer, 0, 0)),
    ]
    args = [zqf, zqf, zig, zig, onorm_a.reshape(N_A, 1, DV_A)]
    state_spec = pl.BlockSpec((None, ns, hb, DK_A, DV_A), lambda b, g, t: (layer, b, g, 0, 0))
    if s0_all is not None:
        in_specs.append(state_spec)
        args.append(s0_all)
    aliases = {}
    for out_idx, prev in ((0, y_prev), (1, states_prev)):
        if prev is not None:
            aliases[len(args)] = out_idx
            in_specs.append(pl.BlockSpec(memory_space=pl.ANY))
            args.append(prev)
    kern = functools.partial(_hgrn_kernel, chunk=chunk, n_chunks=n_chunks, hb=hb, per_batch=per_batch,
                             zero_init=s0_all is None, n_alias=len(aliases))
    return pl.pallas_call(
        kern,
        grid=(nbg, ng, nt),
        in_specs=in_specs,
        out_specs=[
            pl.BlockSpec((tb, wv), lambda b, g, t: (row(b, g, t), g)),
            state_spec,
        ],
        out_shape=[
            jax.ShapeDtypeStruct((n, E_WIDTH), BF16),
            jax.ShapeDtypeStruct((N_A, n_batch, H_A, DK_A, DV_A), F32),
        ],
        scratch_shapes=[
            pltpu.VMEM((ns, hb, DK_A, DV_A), F32),
            pltpu.VMEM((tb, wk), F32),
            pltpu.VMEM((tb, wv), F32),
            pltpu.VMEM((tb, wv), F32),
            pltpu.SMEM((n_chunks,), jnp.int32),
        ],
        input_output_aliases=aliases,
        compiler_params=_params("arbitrary", "arbitrary", "arbitrary"),
        name=f"hgrn_scan_c{chunk}",
    )(*args)


def _swa_prompt_kernel(sink_ref, q_ref, g_ref, kc_ref, kp_ref, vc_ref, vp_ref, y_ref, d_sc, s_sc, p_sc):
    blk = pl.program_id(1)
    kg = pl.program_id(2)
    w = WINDOW
    npair = G_B // 2

    @pl.when(kg == 0)
    def _():
        row = lax.broadcasted_iota(jnp.int32, (w, 2 * w), 0)
        col = lax.broadcasted_iota(jnp.int32, (w, 2 * w), 1)
        dist = w + row - col
        key_pos = (blk - 1) * w + col
        valid = (dist >= 0) & (dist < w) & (key_pos >= PAD)
        d_sc[...] = jnp.where(valid, dist.astype(F32), -NEG_BIG)

    lane = lax.broadcasted_iota(jnp.int32, (w, LANES), 1)
    lane_k = lax.broadcasted_iota(jnp.int32, (2 * w, LANES), 1)
    row_k = lax.broadcasted_iota(jnp.int32, (2 * w, LANES), 0)
    for kk in range(KH_STEP):
        kh = kg * KH_STEP + kk
        kdup = jnp.concatenate([kp_ref[kk], kc_ref[kk]], axis=0)
        k_par = [jnp.where(lane_k < HD_B, kdup, 0.0).astype(BF16), jnp.where(lane_k < HD_B, 0.0, kdup).astype(BF16)]
        vraw = jnp.concatenate([vp_ref[kk], vc_ref[kk]], axis=0)
        vext = jnp.where(row_k == 0, jnp.where(lane_k < HD_B, 0.0, vraw), vraw).astype(BF16)
        qs = jnp.concatenate(
            [q_ref[:, (kk * npair + j) * LANES:(kk * npair + j + 1) * LANES] for j in range(npair)],
            axis=0).astype(BF16)
        for par in range(2):
            s_sc[kk, par] = lax.dot_general(qs, k_par[par], (((1,), (1,)), ((), ())),
                                            preferred_element_type=F32)
        for j in range(npair):
            rows = slice(j * w, (j + 1) * w)
            for par in range(2):
                head = kh * G_B + 2 * j + par
                slope = jnp.exp(-LN2 * 8.0 * (head + 1).astype(F32) / H_B)
                sink = sink_ref[head]
                s_lo = s_sc[kk, par, rows, :w] - slope * d_sc[:, :w]
                s_lo = jnp.where(lane == 0, sink, s_lo)
                s_hi = s_sc[kk, par, rows, w:] - slope * d_sc[:, w:]
                m = jnp.max(jnp.maximum(s_lo, s_hi), axis=-1, keepdims=True)
                p_sc[kk, par, rows, :w] = jnp.exp(s_lo - m).astype(BF16)
                p_sc[kk, par, rows, w:] = jnp.exp(s_hi - m).astype(BF16)
        o_par = [jnp.dot(p_sc[kk, par], vext, preferred_element_type=F32) for par in range(2)]
        for j in range(npair):
            o_even = o_par[0][j * w:(j + 1) * w, :]
            o_odd = o_par[1][j * w:(j + 1) * w, :]
            lo = o_even / pltpu.roll(o_even, HD_B, 1)
            hi = pltpu.roll(o_odd, HD_B, 1) / o_odd
            cs = slice((kk * npair + j) * LANES, (kk * npair + j + 1) * LANES)
            y_ref[:, cs] = (jnp.where(lane < HD_B, lo, hi) * g_ref[:, cs].astype(F32)).astype(BF16)


def swa_prompt(zb, kv_hm, sink, n_batch, blocks_per_batch):
    n = zb.shape[0]
    w = WINDOW
    gw = KH_STEP * G_B * HD_B
    nb = blocks_per_batch
    nkg = KV_B // KH_STEP
    cur = lambda b, i, kg: b * nb + i
    prev = lambda b, i, kg: b * nb + jnp.maximum(i - 1, 0)
    kv_block = (KH_STEP, w, 2 * HD_B)
    return pl.pallas_call(
        _swa_prompt_kernel,
        grid=(n_batch, nb, nkg),
        in_specs=[
            pl.BlockSpec(memory_space=pltpu.SMEM),
            pl.BlockSpec((w, gw), lambda b, i, kg: (cur(b, i, kg), kg)),
            pl.BlockSpec((w, gw), lambda b, i, kg: (cur(b, i, kg), nkg + kg)),
            pl.BlockSpec(kv_block, lambda b, i, kg: (kg, cur(b, i, kg), 0)),
            pl.BlockSpec(kv_block, lambda b, i, kg: (kg, prev(b, i, kg), 0)),
            pl.BlockSpec(kv_block, lambda b, i, kg: (nkg + kg, cur(b, i, kg), 0)),
            pl.BlockSpec(kv_block, lambda b, i, kg: (nkg + kg, prev(b, i, kg), 0)),
        ],
        out_specs=pl.BlockSpec((w, gw), lambda b, i, kg: (cur(b, i, kg), kg)),
        out_shape=jax.ShapeDtypeStruct((n, E_WIDTH), BF16),
        scratch_shapes=[
            pltpu.VMEM((w, 2 * w), F32),
            pltpu.VMEM((KH_STEP, 2, G_B // 2 * w, 2 * w), F32),
            pltpu.VMEM((KH_STEP, 2, G_B // 2 * w, 2 * w), BF16),
        ],
        compiler_params=_params("arbitrary", "arbitrary", "arbitrary"),
        name="swa_prompt",
    )(sink, zb, zb, kv_hm, kv_hm, kv_hm, kv_hm)


def _swa_sample_kernel(sink_ref, q_ref, g_ref, kn_ref, vn_ref, ck_ref, cv_ref, yprev_ref, y_ref, *, t_new, nbs):
    del yprev_ref
    wb = ck_ref.shape[1]
    nk = wb + t_new
    nr = G_B * t_new
    row = lax.broadcasted_iota(jnp.int32, (nr, nk), 0)
    col = lax.broadcasted_iota(jnp.int32, (nr, nk), 1)
    tok = row % t_new
    grp = row // t_new
    dist = wb + tok - col
    valid = (dist >= 0) & (dist < WINDOW)
    distf = dist.astype(F32)
    grp1 = lax.broadcasted_iota(jnp.int32, (nr, 1), 0) // t_new
    q_all = q_ref[...].astype(F32)
    g_all = g_ref[...].astype(F32)
    for kh in range(KV_B):
        ks = slice(kh * HD_B, (kh + 1) * HD_B)
        slope = jnp.exp(-LN2 * 8.0 * (kh * G_B + grp + 1).astype(F32) / H_B)
        sink = jnp.zeros((nr, 1), F32)
        for g in range(G_B):
            sink = jnp.where(grp1 == g, sink_ref[kh * G_B + g], sink)
        outs = []
        for bi in range(nbs):
            rb = slice(bi * t_new, (bi + 1) * t_new)
            kf = jnp.concatenate([ck_ref[bi, :, ks], kn_ref[kh][rb, :HD_B]], axis=0).astype(BF16)
            vf = jnp.concatenate([cv_ref[bi, :, ks], vn_ref[kh][rb, :HD_B]], axis=0).astype(BF16)
            qs = jnp.concatenate(
                [q_all[rb, (kh * G_B + g) * HD_B:(kh * G_B + g + 1) * HD_B] for g in range(G_B)], axis=0)
            s = lax.dot_general(qs.astype(BF16), kf, (((1,), (1,)), ((), ())), preferred_element_type=F32)
            s = s - slope * distf
            s = jnp.where(valid, s, NEG_BIG)
            m = jnp.maximum(jnp.max(s, axis=-1, keepdims=True), sink)
            p = jnp.where(valid, jnp.exp(s - m), 0.0)
            den = jnp.sum(p, axis=-1, keepdims=True) + jnp.exp(sink - m)
            outs.append(jnp.dot(p.astype(BF16), vf, preferred_element_type=F32) / den)
        for g in range(G_B):
            hsl = slice((kh * G_B + g) * HD_B, (kh * G_B + g + 1) * HD_B)
            o = jnp.concatenate([outs[bi][g * t_new:(g + 1) * t_new, :] for bi in range(nbs)], axis=0)
            y_ref[:, hsl] = (o * g_all[:, hsl]).astype(y_ref.dtype)


def swa_sample(zb, kv_hm, sink, y_prev, cache_k, cache_v, *, row0, n_batch, t_new):
    n = zb.shape[0]
    wb = cache_k.shape[1]
    nbs = SAMPLE_BATCHES
    tb = nbs * t_new
    rb0 = row0 // tb
    kern = functools.partial(_swa_sample_kernel, t_new=t_new, nbs=nbs)
    return pl.pallas_call(
        kern,
        grid=(n_batch // nbs,),
        in_specs=[
            pl.BlockSpec(memory_space=pltpu.SMEM),
            pl.BlockSpec((tb, E_WIDTH), lambda b: (rb0 + b, 0)),
            pl.BlockSpec((tb, E_WIDTH), lambda b: (rb0 + b, 1)),
            pl.BlockSpec((KV_B, tb, 2 * HD_B), lambda b: (0, rb0 + b, 0)),
            pl.BlockSpec((KV_B, tb, 2 * HD_B), lambda b: (1, rb0 + b, 0)),
            pl.BlockSpec((nbs, wb, KV_B * HD_B), lambda b: (b, 0, 0)),
            pl.BlockSpec((nbs, wb, KV_B * HD_B), lambda b: (b, 0, 0)),
            pl.BlockSpec(memory_space=pl.ANY),
        ],
        out_specs=pl.BlockSpec((tb, E_WIDTH), lambda b: (rb0 + b, 0)),
        out_shape=jax.ShapeDtypeStruct((n, E_WIDTH), BF16),
        input_output_aliases={7: 0},
        compiler_params=_params("arbitrary"),
        name="swa_sample",
    )(sink, zb, zb, kv_hm, kv_hm, cache_k, cache_v, y_prev)


def kernel(x_prompt, x_sample, state_hgrn, cache_k, cache_v, meta_tokens, norm_a, w_in_a, lb_a, onorm_a,
           w_out_a, norm_kv, w_kv, norm_b, w_in_b, sink_b, w_out_b, norm_f):
    bsz, seq, d = x_prompt.shape
    dec_b, dec_t, _ = x_sample.shape
    wb = cache_k.shape[1]
    w_out_a16 = w_out_a.astype(BF16)
    w_out_b16 = w_out_b.astype(BF16)

    rows_p = PAD + N_META + seq
    nb_p = rows_p // ROW_BLOCK
    n_p = bsz * rows_p
    n_s = dec_b * dec_t
    xp = jnp.concatenate([
        jnp.zeros((bsz, PAD, d), x_prompt.dtype),
        jnp.broadcast_to(meta_tokens.astype(x_prompt.dtype)[None], (bsz, N_META, d)),
        x_prompt], axis=1).reshape(n_p, d)
    x = jnp.concatenate([xp, x_sample.reshape(n_s, d)], axis=0)
    fd = 2 * H_A * DK_A
    s0_s = state_hgrn.astype(F32)
    ck = cache_k.reshape(dec_b, wb, KV_B * HD_B)
    cv = cache_v.reshape(dec_b, wb, KV_B * HD_B)

    xn = prenorm(x, norm_a, 0)
    sp = ss = None
    for layer in range(N_A):
        zqf = proj(xn, w_in_a, layer, 0, ((fd // 2, "silu_dk"), (fd // 2, "logf")), F32, lb_a=lb_a)
        zig = proj(xn, w_in_a, layer, fd, ((E_WIDTH, "id"), (E_WIDTH, "silu")), BF16)
        y, sp = hgrn_scan(zqf, zig, onorm_a, None, None, sp, layer=layer, n_batch=bsz, rows_per_batch=rows_p,
                          row0=0, chunk=64, n_chunks=ROW_BLOCK // 64, hb=16, per_batch=False)
        y, ss = hgrn_scan(zqf, zig, onorm_a, s0_s, y, ss, layer=layer, n_batch=dec_b, rows_per_batch=dec_t,
                          row0=n_p, chunk=dec_t, n_chunks=SAMPLE_BATCHES, hb=8, per_batch=True)
        if layer + 1 < N_A:
            x, xn = matmul_residual(y, w_out_a16, layer, x, (norm_a[layer + 1],), BF16)
        else:
            x, xn, xn_kv = matmul_residual(y, w_out_a16, layer, x, (norm_b[0], norm_kv), BF16)
    kv_hm = kv_proj(xn_kv, w_kv)
    for layer in range(N_B):
        zb = proj(xn, w_in_b, layer, 0, ((E_WIDTH, "scale_hd"), (E_WIDTH, "silu")), BF16)
        y = swa_prompt(zb, kv_hm, sink_b[layer], bsz, nb_p)
        y = swa_sample(zb, kv_hm, sink_b[layer], y, ck, cv, row0=n_p, n_batch=dec_b, t_new=dec_t)
        if layer + 1 < N_B:
            x, xn = matmul_residual(y, w_out_b16, layer, x, (norm_b[layer + 1],), BF16)
        else:
            (out,) = matmul_residual(y, w_out_b16, layer, x, (norm_f,), F32, keep_x=False)

    y_prompt = out[:n_p].reshape(bsz, rows_p, d)[:, PAD + N_META:]
    y_sample = out[n_p:].reshape(dec_b, dec_t, d)
    kvp = kv_hm[:, :n_p].reshape(2, KV_B, bsz, rows_p, 2 * HD_B)[:, :, :, rows_p - wb:, :HD_B]
    cache_k_prompt = jnp.transpose(kvp[0], (1, 2, 0, 3)).astype(cache_k.dtype)
    cache_v_prompt = jnp.transpose(kvp[1], (1, 2, 0, 3)).astype(cache_v.dtype)
    kvs = kv_hm[:, n_p:].reshape(2, KV_B, dec_b, dec_t, 2 * HD_B)[..., :HD_B]
    k_new = jnp.transpose(kvs[0], (1, 2, 0, 3)).astype(cache_k.dtype)
    v_new = jnp.transpose(kvs[1], (1, 2, 0, 3)).astype(cache_v.dtype)
    cache_k_sample = jnp.concatenate([cache_k, k_new], axis=1)[:, -wb:]
    cache_v_sample = jnp.concatenate([cache_v, v_new], axis=1)[:, -wb:]

    return (y_prompt, y_sample, sp.astype(state_hgrn.dtype), cache_k_prompt, cache_v_prompt,
            ss.astype(state_hgrn.dtype), cache_k_sample, cache_v_sample)
```

```python
import functools

import jax
import jax.numpy as jnp
from jax import lax
from jax.experimental import pallas as pl
from jax.experimental.pallas import tpu as pltpu

D_MODEL = 2048
E_WIDTH = 2 * D_MODEL
N_A = 2
N_B = 2
DK_A = 128
H_A = D_MODEL // DK_A
DV_A = E_WIDTH // H_A
HD_B = 64
H_B = E_WIDTH // HD_B
KV_B = H_B // 8
G_B = H_B // KV_B
WINDOW = 128
N_META = 16
EPS = 1e-6
NEG_BIG = -1e30
TINY = 1e-30

LANES = 128
BF16_SUBLANES = 16
ROW_BLOCK = 128
PAD = ROW_BLOCK - N_META
FACTOR_SAFE = 60.0
LN2 = 0.6931471805599453
VMEM_LIMIT = 56 * 1024 * 1024
KH_STEP = 8
SAMPLE_BATCHES = BF16_SUBLANES // 8
PROJ_ROWS = 1120
RES_ROWS = 448
PROJ_COLS = 1024
PROJ_SUBTILES = 5

F32 = jnp.float32
BF16 = jnp.bfloat16


def _sigmoid(x):
    return 1.0 / (1.0 + jnp.exp(-x))


def _silu(x):
    return x * _sigmoid(x)


def _params(*sem):
    return pltpu.CompilerParams(dimension_semantics=sem, vmem_limit_bytes=VMEM_LIMIT)


def _row_tile(n, cap):
    for t in range(min(cap, n) // BF16_SUBLANES * BF16_SUBLANES, 0, -BF16_SUBLANES):
        if n % t == 0:
            return t
    raise ValueError(f"no row tile for {n}")


def _assemble_kernel(xp_ref, meta_ref, xs_ref, g_ref, x_ref, xn_ref, *, n_prompt_blocks, blocks_per_batch):
    r = pl.program_id(0)
    is_prompt = r < n_prompt_blocks
    first = (r % blocks_per_batch) == 0

    def emit(x):
        x_ref[...] = x
        ms = jnp.mean(x * x, axis=-1, keepdims=True)
        xn_ref[...] = (x * lax.rsqrt(ms + EPS) * g_ref[...]).astype(xn_ref.dtype)

    @pl.when(is_prompt & first)
    def _():
        emit(jnp.concatenate([jnp.zeros((PAD, x_ref.shape[1]), F32), meta_ref[...].astype(F32)], axis=0))

    @pl.when(is_prompt & jnp.logical_not(first))
    def _():
        emit(xp_ref[...].astype(F32))

    @pl.when(jnp.logical_not(is_prompt))
    def _():
        emit(xs_ref[...].astype(F32))


def assemble(x_prompt, meta_tokens, x_sample2d, g_all, layer):
    bsz, seq, d = x_prompt.shape
    n_s = x_sample2d.shape[0]
    assert seq % ROW_BLOCK == 0 and n_s % ROW_BLOCK == 0, (seq, n_s)
    nb = 1 + seq // ROW_BLOCK
    npb = bsz * nb
    nsb = n_s // ROW_BLOCK
    n = (npb + nsb) * ROW_BLOCK
    kern = functools.partial(_assemble_kernel, n_prompt_blocks=npb, blocks_per_batch=nb)
    row_spec = pl.BlockSpec((ROW_BLOCK, d), lambda r: (r, 0))
    return pl.pallas_call(
        kern,
        grid=(npb + nsb,),
        in_specs=[
            pl.BlockSpec((None, ROW_BLOCK, d),
                         lambda r: (jnp.minimum(r // nb, bsz - 1), jnp.maximum(r % nb - 1, 0), 0)),
            pl.BlockSpec((N_META, d), lambda r: (0, 0)),
            pl.BlockSpec((ROW_BLOCK, d), lambda r: (jnp.clip(r - npb, 0, nsb - 1), 0)),
            pl.BlockSpec((None, 1, d), lambda r: (layer, 0, 0)),
        ],
        out_specs=[row_spec, row_spec],
        out_shape=[jax.ShapeDtypeStruct((n, d), F32), jax.ShapeDtypeStruct((n, d), BF16)],
        compiler_params=_params("arbitrary"),
        name="assemble",
    )(x_prompt, meta_tokens, x_sample2d, g_all.reshape(g_all.shape[0], 1, d))


def _log_forget(f, lbp, layer):
    e = jnp.exp(lbp - jnp.max(lbp, axis=0, keepdims=True))
    p = e / jnp.sum(e, axis=0, keepdims=True)
    acc = p[0:1]
    for j in range(1, layer + 1):
        acc = acc + p[j:j + 1]
    lb = acc - p[0:1]
    ef = jnp.exp(-jnp.abs(f))
    r = 1.0 / (1.0 + ef)
    sig = jnp.where(f >= 0, 1.0, ef) * r
    pos = lb > 0
    arg = jnp.where(pos, jnp.maximum(lb, TINY) + (1.0 - lb) * sig, r)
    off = jnp.where(pos, 0.0, jnp.log1p(-lb) + jnp.minimum(f, 0.0))
    return jnp.log(arg) + off


def _proj_kernel(*refs, segments, layer):
    if any(mode == "logf" for _, _, mode in segments):
        x_ref, w_ref, lb_ref, o_ref, wb_ref = refs
    else:
        x_ref, w_ref, o_ref, wb_ref = refs
        lb_ref = None
    j = pl.program_id(0)

    @pl.when(pl.program_id(1) == 0)
    def _():
        wb_ref[...] = w_ref[...].astype(BF16)

    for j0, j1, mode in segments:
        @pl.when((j >= j0) & (j < j1))
        def _(mode=mode):
            tm = x_ref.shape[0]
            sub = tm // PROJ_SUBTILES if tm % (PROJ_SUBTILES * BF16_SUBLANES) == 0 else tm
            for r0 in range(0, tm, sub):
                acc = jnp.dot(x_ref[r0:r0 + sub, :], wb_ref[...], preferred_element_type=F32)
                if mode == "silu":
                    acc = _silu(acc)
                elif mode == "silu_dk":
                    acc = _silu(acc) * (DK_A ** -0.5)
                elif mode == "scale_hd":
                    acc = acc * (HD_B ** -0.5)
                elif mode == "logf":
                    acc = _log_forget(acc, lb_ref[...], layer)
                else:
                    assert mode == "id", mode
                o_ref[r0:r0 + sub, :] = acc.astype(o_ref.dtype)


def proj(xn, w_all, layer, col0, segments, out_dtype, lb_a=None):
    n, d = xn.shape
    tn = PROJ_COLS
    tm = _row_tile(n, PROJ_ROWS)
    jb = col0 // tn
    bounds, j0 = [], 0
    for ncols_seg, mode in segments:
        bounds.append((j0, j0 + ncols_seg // tn, mode))
        j0 += ncols_seg // tn
    ncols = j0 * tn
    in_specs = [
        pl.BlockSpec((tm, d), lambda j, i: (i, 0)),
        pl.BlockSpec((None, d, tn), lambda j, i: (layer, 0, jb + j)),
    ]
    args = [xn, w_all]
    for ja, jz, mode in bounds:
        if mode == "logf":
            in_specs.append(pl.BlockSpec((lb_a.shape[0], tn),
                                         lambda j, i, ja=ja, jz=jz: (0, jnp.clip(j - ja, 0, jz - ja - 1))))
            args.append(lb_a)
    kern = functools.partial(_proj_kernel, segments=tuple(bounds), layer=layer)
    return pl.pallas_call(
        kern,
        grid=(ncols // tn, n // tm),
        in_specs=in_specs,
        out_specs=pl.BlockSpec((tm, tn), lambda j, i: (i, j)),
        out_shape=jax.ShapeDtypeStruct((n, ncols), out_dtype),
        scratch_shapes=[pltpu.VMEM((d, tn), BF16)],
        compiler_params=_params("arbitrary", "arbitrary"),
        name="proj",
    )(*args)


def _kv_proj_kernel(x_ref, w_ref, o_ref, wb_ref):
    @pl.when(pl.program_id(0) == 0)
    def _():
        wb_ref[...] = w_ref[...].astype(BF16)

    kv = jnp.dot(x_ref[...], wb_ref[...], preferred_element_type=F32)
    ones = jnp.ones((kv.shape[0], HD_B), F32)
    for j in range(KV_B):
        k_j = kv[:, j * HD_B:(j + 1) * HD_B]
        o_ref[j] = jnp.concatenate([k_j, k_j], axis=1)
        v_j = kv[:, (KV_B + j) * HD_B:(KV_B + j + 1) * HD_B]
        o_ref[KV_B + j] = jnp.concatenate([v_j, ones], axis=1)


def kv_proj(xn, w):
    n, d = xn.shape
    tm = _row_tile(n, RES_ROWS)
    return pl.pallas_call(
        _kv_proj_kernel,
        grid=(n // tm,),
        in_specs=[
            pl.BlockSpec((tm, d), lambda i: (i, 0)),
            pl.BlockSpec((d, 2 * KV_B * HD_B), lambda i: (0, 0), pipeline_mode=pl.Buffered(1)),
        ],
        out_specs=pl.BlockSpec((2 * KV_B, tm, 2 * HD_B), lambda i: (0, i, 0)),
        out_shape=jax.ShapeDtypeStruct((2 * KV_B, n, 2 * HD_B), F32),
        scratch_shapes=[pltpu.VMEM((d, 2 * KV_B * HD_B), BF16)],
        compiler_params=_params("arbitrary"),
        name="kv_proj",
    )(xn, w)


def _matmul_res_kernel(y_ref, w_ref, x_ref, g_ref, *out_refs, n_norm, keep_x):
    x = x_ref[...] + jnp.dot(y_ref[...], w_ref[...], preferred_element_type=F32)
    pos = 0
    if keep_x:
        out_refs[0][...] = x
        pos = 1
    ms = jnp.mean(x * x, axis=-1, keepdims=True)
    xh = x * lax.rsqrt(ms + EPS)
    for t in range(n_norm):
        out_refs[pos + t][...] = (xh * g_ref[t]).astype(out_refs[pos + t].dtype)


def matmul_residual(y, w_all, layer, x, gains, norm_dtype, keep_x=True):
    n, e = y.shape
    d = w_all.shape[2]
    tm = _row_tile(n, RES_ROWS)
    gs = jnp.stack([g.reshape(1, d) for g in gains])
    row_spec = pl.BlockSpec((tm, d), lambda i: (i, 0))
    out_shapes = ([jax.ShapeDtypeStruct((n, d), F32)] if keep_x else []) + \
        [jax.ShapeDtypeStruct((n, d), norm_dtype) for _ in gains]
    kern = functools.partial(_matmul_res_kernel, n_norm=len(gains), keep_x=keep_x)
    return pl.pallas_call(
        kern,
        grid=(n // tm,),
        in_specs=[
            pl.BlockSpec((tm, e), lambda i: (i, 0)),
            pl.BlockSpec((None, e, d), lambda i: (layer, 0, 0), pipeline_mode=pl.Buffered(1)),
            row_spec,
            pl.BlockSpec((len(gains), 1, d), lambda i: (0, 0, 0)),
        ],
        out_specs=[row_spec] * len(out_shapes),
        out_shape=out_shapes,
        compiler_params=_params("arbitrary"),
        name="matmul_residual",
    )(y, w_all, x, gs)


def _hgrn_kernel(*refs, chunk, n_chunks, hb, per_batch, zero_init, n_alias):
    refs = list(refs)
    zq_ref, zf_ref, zi_ref, zg_ref, on_ref = refs[:5]
    pos = 5
    s0_ref = None
    if not zero_init:
        s0_ref = refs[pos]
        pos += 1
    pos += n_alias
    y_ref, sfin_ref, s_ref, cum_sc, oint_sc, ystage_sc, flag_ref = refs[pos:]
    c = chunk
    t = pl.program_id(2)
    aligned = c % BF16_SUBLANES == 0

    @pl.when(t == 0)
    def _():
        if zero_init:
            s_ref[...] = jnp.zeros_like(s_ref)
        else:
            s_ref[...] = s0_ref[...]

    on_g = on_ref[...]
    if aligned:
        zi_all = zg_all = None
    else:
        zi_all = zi_ref[...].astype(F32)
        zg_all = zg_ref[...].astype(F32)

    ri = lax.broadcasted_iota(jnp.int32, (c, c), 0)
    ci_ = lax.broadcasted_iota(jnp.int32, (c, c), 1)
    tril = (ci_ <= ri).astype(BF16)
    row_c = lax.broadcasted_iota(jnp.int32, (c, LANES), 0)
    col_c = lax.broadcasted_iota(jnp.int32, (c, LANES), 1)
    causal = col_c <= row_c
    zpad_k = jnp.zeros((LANES - c, DK_A), F32)
    zpad_x = jnp.zeros((LANES - c - 8, DK_A), F32)

    def values(rows, vs_):
        return zi_ref[rows, vs_].astype(F32) if aligned else zi_all[rows, vs_]

    def values_padded(rows, vs_):
        if aligned:
            return jnp.concatenate([zi_ref[rows, vs_], jnp.zeros((LANES - c, DV_A), BF16)], axis=0)
        return jnp.concatenate([zi_all[rows, vs_], jnp.zeros((LANES - c, DV_A), F32)], axis=0).astype(BF16)

    def emit_y(o, rows, vs_):
        ms = jnp.mean(o * o, axis=-1, keepdims=True)
        o = o * lax.rsqrt(ms + EPS) * on_g
        if aligned:
            y_ref[rows, vs_] = (o * zg_ref[rows, vs_].astype(F32)).astype(y_ref.dtype)
        else:
            ystage_sc[rows, vs_] = o * zg_all[rows, vs_]

    for ci in range(n_chunks):
        rows = slice(ci * c, (ci + 1) * c)
        si = ci if per_batch else 0
        logf = zf_ref[rows, :]
        h1 = logf.astype(BF16)
        h2 = (logf - h1.astype(F32)).astype(BF16)
        cum = jnp.dot(tril, h1, preferred_element_type=F32) + jnp.dot(tril, h2, preferred_element_type=F32)
        last = cum[c - 1:c, :]
        mid = cum[c // 2 - 1:c // 2, :]
        dev = jnp.max(jnp.maximum(cum[0:1, :] - mid, mid - last))
        flag_ref[ci] = (dev <= FACTOR_SAFE).astype(jnp.int32)
        cum_sc[rows, :] = cum

        for h in range(hb):
            hs = slice(h * DK_A, (h + 1) * DK_A)
            vs_ = slice(h * DV_A, (h + 1) * DV_A)
            q_h = zq_ref[rows, hs]
            k_h = 1.0 - jnp.exp(zf_ref[rows, hs])
            cum_h = cum_sc[rows, hs]
            v_pad = values_padded(rows, vs_)
            d_h = cum_h - mid[:, hs]
            qm = (q_h * jnp.exp(d_h)).astype(BF16)
            km = jnp.concatenate([k_h * jnp.exp(-d_h), zpad_k], axis=0).astype(BF16)
            att = lax.dot_general(qm, km, (((1,), (1,)), ((), ())), preferred_element_type=F32)
            att = jnp.where(causal, att, 0.0).astype(BF16)
            o_intra = jnp.dot(att, v_pad, preferred_element_type=F32)
            s_prev = s_ref[si, h]
            qa = (q_h * jnp.exp(cum_h)).astype(BF16)
            o_inter = jnp.dot(qa, s_prev.astype(BF16), preferred_element_type=F32)
            oint_sc[rows, vs_] = o_inter
            kd = k_h * jnp.exp(last[:, hs] - cum_h)
            el8 = jnp.broadcast_to(jnp.exp(last[:, hs]), (8, DK_A))
            x_t = jnp.concatenate([kd, el8, zpad_x], axis=0).T
            s_ref[si, h] = x_t[:, c:c + 1] * s_prev + jnp.dot(x_t.astype(BF16), v_pad, preferred_element_type=F32)
            emit_y(o_inter + o_intra, rows, vs_)

    for ci in range(n_chunks):
        rows = slice(ci * c, (ci + 1) * c)

        @pl.when(flag_ref[ci] == 0)
        def _(rows=rows):
            row_v = lax.broadcasted_iota(jnp.int32, (c, 1), 0)

            def row_of(a, s):
                return jnp.sum(jnp.where(row_v == s, a, 0.0), axis=0, keepdims=True)

            for h in range(hb):
                hs = slice(h * DK_A, (h + 1) * DK_A)
                vs_ = slice(h * DV_A, (h + 1) * DV_A)
                q_h = zq_ref[rows, hs]
                k_h = 1.0 - jnp.exp(zf_ref[rows, hs])
                cum_h = cum_sc[rows, hs]
                v_h = values(rows, vs_)

                def body(s, acc_o, q_h=q_h, k_h=k_h, cum_h=cum_h, v_h=v_h):
                    dec = jnp.exp(jnp.minimum(cum_h - row_of(cum_h, s), 0.0))
                    w = jnp.sum(q_h * row_of(k_h, s) * dec, axis=-1, keepdims=True)
                    w = jnp.where(row_v >= s, w, 0.0)
                    return acc_o + w * row_of(v_h, s)

                o_intra = lax.fori_loop(0, c, body, jnp.zeros((c, DV_A), F32))
                emit_y(oint_sc[rows, vs_] + o_intra, rows, vs_)

    if not aligned:
        y_ref[...] = ystage_sc[...].astype(y_ref.dtype)

    @pl.when(t == pl.num_programs(2) - 1)
    def _():
        sfin_ref[...] = s_ref[...]


def hgrn_scan(zqf, zig, onorm_a, s0_all, y_prev, states_prev, *, layer, n_batch, rows_per_batch, row0, chunk,
              n_chunks, hb, per_batch):
    n = zqf.shape[0]
    tb = chunk * n_chunks
    ns = n_chunks if per_batch else 1
    nt = 1 if per_batch else rows_per_batch // tb
    nbg = n_batch // ns
    ng = H_A // hb
    wk, wv = hb * DK_A, hb * DV_A
    f_off = (H_A * DK_A) // wk
    g_off = E_WIDTH // wv
    rb0 = row0 // tb
    row = lambda b, g, t: rb0 + b * nt + t
    in_specs = [
        pl.BlockSpec((tb, wk), lambda b, g, t: (row(b, g, t), g)),
        pl.BlockSpec((tb, wk), lambda b, g, t: (row(b, g, t), f_off + g)),
        pl.BlockSpec((tb, wv), lambda b, g, t: (row(b, g, t), g)),
        pl.BlockSpec((tb, wv), lambda b, g, t: (row(b, g, t), g_off + g)),
        pl.BlockSpec((None, 1, DV_A), lambda b, g, t: (layer, 0, 0)),
    ]
    args = [zqf, zqf, zig, zig, onorm_a.reshape(N_A, 1, DV_A)]
    state_spec = pl.BlockSpec((None, ns, hb, DK_A, DV_A), lambda b, g, t: (layer, b, g, 0, 0))
    if s0_all is not None:
        in_specs.append(state_spec)
        args.append(s0_all)
    aliases = {}
    for out_idx, prev in ((0, y_prev), (1, states_prev)):
        if prev is not None:
            aliases[len(args)] = out_idx
            in_specs.append(pl.BlockSpec(memory_space=pl.ANY))
            args.append(prev)
    kern = functools.partial(_hgrn_kernel, chunk=chunk, n_chunks=n_chunks, hb=hb, per_batch=per_batch,
                             zero_init=s0_all is None, n_alias=len(aliases))
    return pl.pallas_call(
        kern,
        grid=(nbg, ng, nt),
        in_specs=in_specs,
        out_specs=[
            pl.BlockSpec((tb, wv), lambda b, g, t: (row(b, g, t), g)),
            state_spec,
        ],
        out_shape=[
            jax.ShapeDtypeStruct((n, E_WIDTH), BF16),
            jax.ShapeDtypeStruct((N_A, n_batch, H_A, DK_A, DV_A), F32),
        ],
        scratch_shapes=[
            pltpu.VMEM((ns, hb, DK_A, DV_A), F32),
            pltpu.VMEM((tb, wk), F32),
            pltpu.VMEM((tb, wv), F32),
            pltpu.VMEM((tb, wv), F32),
            pltpu.SMEM((n_chunks,), jnp.int32),
        ],
        input_output_aliases=aliases,
        compiler_params=_params("arbitrary", "arbitrary", "arbitrary"),
        name=f"hgrn_scan_c{chunk}",
    )(*args)


def _swa_prompt_kernel(sink_ref, q_ref, g_ref, kc_ref, kp_ref, vc_ref, vp_ref, y_ref, d_sc, s_sc, p_sc):
    blk = pl.program_id(1)
    kg = pl.program_id(2)
    w = WINDOW
    npair = G_B // 2

    @pl.when(kg == 0)
    def _():
        row = lax.broadcasted_iota(jnp.int32, (w, 2 * w), 0)
        col = lax.broadcasted_iota(jnp.int32, (w, 2 * w), 1)
        dist = w + row - col
        key_pos = (blk - 1) * w + col
        valid = (dist >= 0) & (dist < w) & (key_pos >= PAD)
        d_sc[...] = jnp.where(valid, dist.astype(F32), -NEG_BIG)

    lane = lax.broadcasted_iota(jnp.int32, (w, LANES), 1)
    lane_k = lax.broadcasted_iota(jnp.int32, (2 * w, LANES), 1)
    row_k = lax.broadcasted_iota(jnp.int32, (2 * w, LANES), 0)
    for kk in range(KH_STEP):
        kh = kg * KH_STEP + kk
        kdup = jnp.concatenate([kp_ref[kk], kc_ref[kk]], axis=0)
        k_par = [jnp.where(lane_k < HD_B, kdup, 0.0).astype(BF16), jnp.where(lane_k < HD_B, 0.0, kdup).astype(BF16)]
        vraw = jnp.concatenate([vp_ref[kk], vc_ref[kk]], axis=0)
        vext = jnp.where(row_k == 0, jnp.where(lane_k < HD_B, 0.0, vraw), vraw).astype(BF16)
        qs = jnp.concatenate(
            [q_ref[:, (kk * npair + j) * LANES:(kk * npair + j + 1) * LANES] for j in range(npair)],
            axis=0).astype(BF16)
        for par in range(2):
            s_sc[kk, par] = lax.dot_general(qs, k_par[par], (((1,), (1,)), ((), ())),
                                            preferred_element_type=F32)
        for j in range(npair):
            rows = slice(j * w, (j + 1) * w)
            for par in range(2):
                head = kh * G_B + 2 * j + par
                slope = jnp.exp(-LN2 * 8.0 * (head + 1).astype(F32) / H_B)
                sink = sink_ref[head]
                s_lo = s_sc[kk, par, rows, :w] - slope * d_sc[:, :w]
                s_lo = jnp.where(lane == 0, sink, s_lo)
                s_hi = s_sc[kk, par, rows, w:] - slope * d_sc[:, w:]
                m = jnp.max(jnp.maximum(s_lo, s_hi), axis=-1, keepdims=True)
                p_sc[kk, par, rows, :w] = jnp.exp(s_lo - m).astype(BF16)
                p_sc[kk, par, rows, w:] = jnp.exp(s_hi - m).astype(BF16)
        o_par = [jnp.dot(p_sc[kk, par], vext, preferred_element_type=F32) for par in range(2)]
        for j in range(npair):
            o_even = o_par[0][j * w:(j + 1) * w, :]
            o_odd = o_par[1][j * w:(j + 1) * w, :]
            lo = o_even / pltpu.roll(o_even, HD_B, 1)
            hi = pltpu.roll(o_odd, HD_B, 1) / o_odd
            cs = slice((kk * npair + j) * LANES, (kk * npair + j + 1) * LANES)
            y_ref[:, cs] = (jnp.where(lane < HD_B, lo, hi) * g_ref[:, cs].astype(F32)).astype(BF16)


def swa_prompt(zb, kv_hm, sink, n_batch, blocks_per_batch):
    n = zb.shape[0]
    w = WINDOW
    gw = KH_STEP * G_B * HD_B
    nb = blocks_per_batch
    nkg = KV_B // KH_STEP
    cur = lambda b, i, kg: b * nb + i
    prev = lambda b, i, kg: b * nb + jnp.maximum(i - 1, 0)
    kv_block = (KH_STEP, w, 2 * HD_B)
    return pl.pallas_call(
        _swa_prompt_kernel,
        grid=(n_batch, nb, nkg),
        in_specs=[
            pl.BlockSpec(memory_space=pltpu.SMEM),
            pl.BlockSpec((w, gw), lambda b, i, kg: (cur(b, i, kg), kg)),
            pl.BlockSpec((w, gw), lambda b, i, kg: (cur(b, i, kg), nkg + kg)),
            pl.BlockSpec(kv_block, lambda b, i, kg: (kg, cur(b, i, kg), 0)),
            pl.BlockSpec(kv_block, lambda b, i, kg: (kg, prev(b, i, kg), 0)),
            pl.BlockSpec(kv_block, lambda b, i, kg: (nkg + kg, cur(b, i, kg), 0)),
            pl.BlockSpec(kv_block, lambda b, i, kg: (nkg + kg, prev(b, i, kg), 0)),
        ],
        out_specs=pl.BlockSpec((w, gw), lambda b, i, kg: (cur(b, i, kg), kg)),
        out_shape=jax.ShapeDtypeStruct((n, E_WIDTH), BF16),
        scratch_shapes=[
            pltpu.VMEM((w, 2 * w), F32),
            pltpu.VMEM((KH_STEP, 2, G_B // 2 * w, 2 * w), F32),
            pltpu.VMEM((KH_STEP, 2, G_B // 2 * w, 2 * w), BF16),
        ],
        compiler_params=_params("arbitrary", "arbitrary", "arbitrary"),
        name="swa_prompt",
    )(sink, zb, zb, kv_hm, kv_hm, kv_hm, kv_hm)


def _swa_sample_kernel(sink_ref, q_ref, g_ref, kn_ref, vn_ref, ck_ref, cv_ref, yprev_ref, y_ref, *, t_new, nbs):
    del yprev_ref
    wb = ck_ref.shape[1]
    nk = wb + t_new
    nr = G_B * t_new
    row = lax.broadcasted_iota(jnp.int32, (nr, nk), 0)
    col = lax.broadcasted_iota(jnp.int32, (nr, nk), 1)
    tok = row % t_new
    grp = row // t_new
    dist = wb + tok - col
    valid = (dist >= 0) & (dist < WINDOW)
    distf = dist.astype(F32)
    grp1 = lax.broadcasted_iota(jnp.int32, (nr, 1), 0) // t_new
    q_all = q_ref[...].astype(F32)
    g_all = g_ref[...].astype(F32)
    for kh in range(KV_B):
        ks = slice(kh * HD_B, (kh + 1) * HD_B)
        slope = jnp.exp(-LN2 * 8.0 * (kh * G_B + grp + 1).astype(F32) / H_B)
        sink = jnp.zeros((nr, 1), F32)
        for g in range(G_B):
            sink = jnp.where(grp1 == g, sink_ref[kh * G_B + g], sink)
        outs = []
        for bi in range(nbs):
            rb = slice(bi * t_new, (bi + 1) * t_new)
            kf = jnp.concatenate([ck_ref[bi, :, ks], kn_ref[kh][rb, :HD_B]], axis=0).astype(BF16)
            vf = jnp.concatenate([cv_ref[bi, :, ks], vn_ref[kh][rb, :HD_B]], axis=0).astype(BF16)
            qs = jnp.concatenate(
                [q_all[rb, (kh * G_B + g) * HD_B:(kh * G_B + g + 1) * HD_B] for g in range(G_B)], axis=0)
            s = lax.dot_general(qs.astype(BF16), kf, (((1,), (1,)), ((), ())), preferred_element_type=F32)
            s = s - slope * distf
            s = jnp.where(valid, s, NEG_BIG)
            m = jnp.maximum(jnp.max(s, axis=-1, keepdims=True), sink)
            p = jnp.where(valid, jnp.exp(s - m), 0.0)
            den = jnp.sum(p, axis=-1, keepdims=True) + jnp.exp(sink - m)
            outs.append(jnp.dot(p.astype(BF16), vf, preferred_element_type=F32) / den)
        for g in range(G_B):
            hsl = slice((kh * G_B + g) * HD_B, (kh * G_B + g + 1) * HD_B)
            o = jnp.concatenate([outs[bi][g * t_new:(g + 1) * t_new, :] for bi in range(nbs)], axis=0)
            y_ref[:, hsl] = (o * g_all[:, hsl]).astype(y_ref.dtype)


def swa_sample(zb, kv_hm, sink, y_prev, cache_k, cache_v, *, row0, n_batch, t_new):
    n = zb.shape[0]
    wb = cache_k.shape[1]
    nbs = SAMPLE_BATCHES
    tb = nbs * t_new
    rb0 = row0 // tb
    kern = functools.partial(_swa_sample_kernel, t_new=t_new, nbs=nbs)
    return pl.pallas_call(
        kern,
        grid=(n_batch // nbs,),
        in_specs=[
            pl.BlockSpec(memory_space=pltpu.SMEM),
            pl.BlockSpec((tb, E_WIDTH), lambda b: (rb0 + b, 0)),
            pl.BlockSpec((tb, E_WIDTH), lambda b: (rb0 + b, 1)),
            pl.BlockSpec((KV_B, tb, 2 * HD_B), lambda b: (0, rb0 + b, 0)),
            pl.BlockSpec((KV_B, tb, 2 * HD_B), lambda b: (1, rb0 + b, 0)),
            pl.BlockSpec((nbs, wb, KV_B * HD_B), lambda b: (b, 0, 0)),
            pl.BlockSpec((nbs, wb, KV_B * HD_B), lambda b: (b, 0, 0)),
            pl.BlockSpec(memory_space=pl.ANY),
        ],
        out_specs=pl.BlockSpec((tb, E_WIDTH), lambda b: (rb0 + b, 0)),
        out_shape=jax.ShapeDtypeStruct((n, E_WIDTH), BF16),
        input_output_aliases={7: 0},
        compiler_params=_params("arbitrary"),
        name="swa_sample",
    )(sink, zb, zb, kv_hm, kv_hm, cache_k, cache_v, y_prev)


def kernel(x_prompt, x_sample, state_hgrn, cache_k, cache_v, meta_tokens, norm_a, w_in_a, lb_a, onorm_a,
           w_out_a, norm_kv, w_kv, norm_b, w_in_b, sink_b, w_out_b, norm_f):
    bsz, seq, d = x_prompt.shape
    dec_b, dec_t, _ = x_sample.shape
    wb = cache_k.shape[1]
    w_out_a16 = w_out_a.astype(BF16)
    w_out_b16 = w_out_b.astype(BF16)

    rows_p = PAD + N_META + seq
    nb_p = rows_p // ROW_BLOCK
    n_p = bsz * rows_p
    n_s = dec_b * dec_t
    fd = 2 * H_A * DK_A
    s0_s = state_hgrn.astype(F32)
    ck = cache_k.reshape(dec_b, wb, KV_B * HD_B)
    cv = cache_v.reshape(dec_b, wb, KV_B * HD_B)

    x, xn = assemble(x_prompt, meta_tokens, x_sample.reshape(n_s, d), norm_a, 0)
    sp = ss = None
    for layer in range(N_A):
        zqf = proj(xn, w_in_a, layer, 0, ((fd // 2, "silu_dk"), (fd // 2, "logf")), F32, lb_a=lb_a)
        zig = proj(xn, w_in_a, layer, fd, ((E_WIDTH, "id"), (E_WIDTH, "silu")), BF16)
        y, sp = hgrn_scan(zqf, zig, onorm_a, None, None, sp, layer=layer, n_batch=bsz, rows_per_batch=rows_p,
                          row0=0, chunk=64, n_chunks=ROW_BLOCK // 64, hb=16, per_batch=False)
        y, ss = hgrn_scan(zqf, zig, onorm_a, s0_s, y, ss, layer=layer, n_batch=dec_b, rows_per_batch=dec_t,
                          row0=n_p, chunk=dec_t, n_chunks=SAMPLE_BATCHES, hb=8, per_batch=True)
        if layer + 1 < N_A:
            x, xn = matmul_residual(y, w_out_a16, layer, x, (norm_a[layer + 1],), BF16)
        else:
            x, xn, xn_kv = matmul_residual(y, w_out_a16, layer, x, (norm_b[0], norm_kv), BF16)
    kv_hm = kv_proj(xn_kv, w_kv)
    for layer in range(N_B):
        zb = proj(xn, w_in_b, layer, 0, ((E_WIDTH, "scale_hd"), (E_WIDTH, "silu")), BF16)
        y = swa_prompt(zb, kv_hm, sink_b[layer], bsz, nb_p)
        y = swa_sample(zb, kv_hm, sink_b[layer], y, ck, cv, row0=n_p, n_batch=dec_b, t_new=dec_t)
        if layer + 1 < N_B:
            x, xn = matmul_residual(y, w_out_b16, layer, x, (norm_b[layer + 1],), BF16)
        else:
            (out,) = matmul_residual(y, w_out_b16, layer, x, (norm_f,), F32, keep_x=False)

    y_prompt = jnp.stack([out[b * rows_p + PAD + N_META:(b + 1) * rows_p] for b in range(bsz)])
    y_sample = out[n_p:].reshape(dec_b, dec_t, d)
    kvp = jnp.stack([kv_hm[:, (b + 1) * rows_p - wb:(b + 1) * rows_p, :HD_B] for b in range(bsz)], axis=1)
    kvp = kvp.reshape(2, KV_B, bsz, wb, HD_B)
    cache_k_prompt = jnp.transpose(kvp[0], (1, 2, 0, 3)).astype(cache_k.dtype)
    cache_v_prompt = jnp.transpose(kvp[1], (1, 2, 0, 3)).astype(cache_v.dtype)
    kvs = kv_hm[:, n_p:, :HD_B].reshape(2, KV_B, dec_b, dec_t, HD_B)
    k_new = jnp.transpose(kvs[0], (1, 2, 0, 3)).astype(cache_k.dtype)
    v_new = jnp.transpose(kvs[1], (1, 2, 0, 3)).astype(cache_v.dtype)
    cache_k_sample = jnp.concatenate([cache_k, k_new], axis=1)[:, -wb:]
    cache_v_sample = jnp.concatenate([cache_v, v_new], axis=1)[:, -wb:]

    return (y_prompt, y_sample, sp.astype(state_hgrn.dtype), cache_k_prompt, cache_v_prompt,
            ss.astype(state_hgrn.dtype), cache_k_sample, cache_v_sample)
```

```python
import functools
import math

import jax
import jax.numpy as jnp
from jax import lax
from jax.experimental import pallas as pl
from jax.experimental.pallas import tpu as pltpu

D_MODEL = 2048
E_WIDTH = 2 * D_MODEL
N_A = 2
N_B = 2
DK_A = 128
H_A = D_MODEL // DK_A
DV_A = E_WIDTH // H_A
HD_B = 64
H_B = E_WIDTH // HD_B
KV_B = H_B // 8
G_B = H_B // KV_B
WINDOW = 128
N_META = 16
EPS = 1e-6
NEG_BIG = -1e30
TINY = 1e-30

LANES = 128
BF16_SUBLANES = 16
ROW_BLOCK = 128
PAD = ROW_BLOCK - N_META
FACTOR_SAFE = 60.0
LN2 = 0.6931471805599453
VMEM_LIMIT = 56 * 1024 * 1024
KH_STEP = 8
SAMPLE_BATCHES = BF16_SUBLANES // 8
PROJ_ROWS = 1120
RES_ROWS = 448
PROJ_COLS = 1024
PROJ_SUBTILES = 5

F32 = jnp.float32
BF16 = jnp.bfloat16


def _sigmoid(x):
    return 1.0 / (1.0 + jnp.exp(-x))


def _silu(x):
    return x * _sigmoid(x)


def _params(*sem):
    return pltpu.CompilerParams(dimension_semantics=sem, vmem_limit_bytes=VMEM_LIMIT)


def _row_tile(n, cap):
    for t in range(min(cap, n) // BF16_SUBLANES * BF16_SUBLANES, 0, -BF16_SUBLANES):
        if n % t == 0:
            return t
    raise ValueError(f"no row tile for {n}")


def _assemble_kernel(xp_ref, meta_ref, xs_ref, g_ref, x_ref, xn_ref, *, n_prompt_blocks, blocks_per_batch):
    r = pl.program_id(0)
    is_prompt = r < n_prompt_blocks
    first = (r % blocks_per_batch) == 0

    def emit(x):
        x_ref[...] = x
        ms = jnp.mean(x * x, axis=-1, keepdims=True)
        xn_ref[...] = (x * lax.rsqrt(ms + EPS) * g_ref[...]).astype(xn_ref.dtype)

    @pl.when(is_prompt & first)
    def _():
        emit(jnp.concatenate([jnp.zeros((PAD, x_ref.shape[1]), F32), meta_ref[...].astype(F32)], axis=0))

    @pl.when(is_prompt & jnp.logical_not(first))
    def _():
        emit(xp_ref[...].astype(F32))

    @pl.when(jnp.logical_not(is_prompt))
    def _():
        emit(xs_ref[...].astype(F32))


def assemble(x_prompt, meta_tokens, x_sample2d, g_all, layer):
    bsz, seq, d = x_prompt.shape
    n_s = x_sample2d.shape[0]
    assert seq % ROW_BLOCK == 0 and n_s % ROW_BLOCK == 0, (seq, n_s)
    nb = 1 + seq // ROW_BLOCK
    npb = bsz * nb
    nsb = n_s // ROW_BLOCK
    n = (npb + nsb) * ROW_BLOCK
    kern = functools.partial(_assemble_kernel, n_prompt_blocks=npb, blocks_per_batch=nb)
    row_spec = pl.BlockSpec((ROW_BLOCK, d), lambda r: (r, 0))
    return pl.pallas_call(
        kern,
        grid=(npb + nsb,),
        in_specs=[
            pl.BlockSpec((None, ROW_BLOCK, d),
                         lambda r: (jnp.minimum(r // nb, bsz - 1), jnp.maximum(r % nb - 1, 0), 0)),
            pl.BlockSpec((N_META, d), lambda r: (0, 0)),
            pl.BlockSpec((ROW_BLOCK, d), lambda r: (jnp.clip(r - npb, 0, nsb - 1), 0)),
            pl.BlockSpec((None, 1, d), lambda r: (layer, 0, 0)),
        ],
        out_specs=[row_spec, row_spec],
        out_shape=[jax.ShapeDtypeStruct((n, d), F32), jax.ShapeDtypeStruct((n, d), BF16)],
        compiler_params=_params("arbitrary"),
        name="assemble",
    )(x_prompt, meta_tokens, x_sample2d, g_all.reshape(g_all.shape[0], 1, d))


def _log_forget(f, lbp, layer):
    e = jnp.exp(lbp - jnp.max(lbp, axis=0, keepdims=True))
    p = e / jnp.sum(e, axis=0, keepdims=True)
    acc = p[0:1]
    for j in range(1, layer + 1):
        acc = acc + p[j:j + 1]
    lb = acc - p[0:1]
    ef = jnp.exp(-jnp.abs(f))
    r = 1.0 / (1.0 + ef)
    sig = jnp.where(f >= 0, 1.0, ef) * r
    pos = lb > 0
    arg = jnp.where(pos, jnp.maximum(lb, TINY) + (1.0 - lb) * sig, r)
    off = jnp.where(pos, 0.0, jnp.log1p(-lb) + jnp.minimum(f, 0.0))
    return jnp.log(arg) + off


def _proj_kernel(*refs, segments, layer):
    if any(mode == "logf" for _, _, mode in segments):
        x_ref, w_ref, lb_ref, o_ref, wb_ref = refs
    else:
        x_ref, w_ref, o_ref, wb_ref = refs
        lb_ref = None
    j = pl.program_id(0)

    @pl.when(pl.program_id(1) == 0)
    def _():
        wb_ref[...] = w_ref[...].astype(BF16)

    for j0, j1, mode in segments:
        @pl.when((j >= j0) & (j < j1))
        def _(mode=mode):
            tm = x_ref.shape[0]
            nsub = PROJ_SUBTILES * max(tm // PROJ_ROWS, 1)
            sub = tm // nsub if tm % (nsub * BF16_SUBLANES) == 0 else tm
            for r0 in range(0, tm, sub):
                acc = jnp.dot(x_ref[r0:r0 + sub, :], wb_ref[...], preferred_element_type=F32)
                if mode == "silu":
                    acc = _silu(acc)
                elif mode == "silu_dk":
                    acc = _silu(acc) * (DK_A ** -0.5)
                elif mode == "scale_hd":
                    acc = acc * (HD_B ** -0.5)
                elif mode == "logf":
                    acc = _log_forget(acc, lb_ref[...], layer)
                else:
                    assert mode == "id", mode
                o_ref[r0:r0 + sub, :] = acc.astype(o_ref.dtype)


def proj(xn, w_all, layer, col0, segments, out_dtype, lb_a=None):
    n, d = xn.shape
    tn = PROJ_COLS
    tm = _row_tile(n, PROJ_ROWS * (4 // jnp.dtype(out_dtype).itemsize))
    jb = col0 // tn
    bounds, j0 = [], 0
    for ncols_seg, mode in segments:
        bounds.append((j0, j0 + ncols_seg // tn, mode))
        j0 += ncols_seg // tn
    ncols = j0 * tn
    in_specs = [
        pl.BlockSpec((tm, d), lambda j, i: (i, 0)),
        pl.BlockSpec((None, d, tn), lambda j, i: (layer, 0, jb + j)),
    ]
    args = [xn, w_all]
    for ja, jz, mode in bounds:
        if mode == "logf":
            in_specs.append(pl.BlockSpec((lb_a.shape[0], tn),
                                         lambda j, i, ja=ja, jz=jz: (0, jnp.clip(j - ja, 0, jz - ja - 1))))
            args.append(lb_a)
    kern = functools.partial(_proj_kernel, segments=tuple(bounds), layer=layer)
    return pl.pallas_call(
        kern,
        grid=(ncols // tn, n // tm),
        in_specs=in_specs,
        out_specs=pl.BlockSpec((tm, tn), lambda j, i: (i, j)),
        out_shape=jax.ShapeDtypeStruct((n, ncols), out_dtype),
        scratch_shapes=[pltpu.VMEM((d, tn), BF16)],
        compiler_params=_params("arbitrary", "arbitrary"),
        name="proj",
    )(*args)


def _kv_proj_kernel(x_ref, w_ref, o_ref, wb_ref):
    @pl.when(pl.program_id(0) == 0)
    def _():
        wb_ref[...] = w_ref[...].astype(BF16)

    kv = jnp.dot(x_ref[...], wb_ref[...], preferred_element_type=F32)
    ones = jnp.ones((kv.shape[0], HD_B), F32)
    for j in range(KV_B):
        k_j = kv[:, j * HD_B:(j + 1) * HD_B]
        o_ref[j] = jnp.concatenate([k_j, k_j], axis=1)
        v_j = kv[:, (KV_B + j) * HD_B:(KV_B + j + 1) * HD_B]
        o_ref[KV_B + j] = jnp.concatenate([v_j, ones], axis=1)


def kv_proj(xn, w):
    n, d = xn.shape
    tm = _row_tile(n, RES_ROWS)
    return pl.pallas_call(
        _kv_proj_kernel,
        grid=(n // tm,),
        in_specs=[
            pl.BlockSpec((tm, d), lambda i: (i, 0)),
            pl.BlockSpec((d, 2 * KV_B * HD_B), lambda i: (0, 0), pipeline_mode=pl.Buffered(1)),
        ],
        out_specs=pl.BlockSpec((2 * KV_B, tm, 2 * HD_B), lambda i: (0, i, 0)),
        out_shape=jax.ShapeDtypeStruct((2 * KV_B, n, 2 * HD_B), F32),
        scratch_shapes=[pltpu.VMEM((d, 2 * KV_B * HD_B), BF16)],
        compiler_params=_params("arbitrary"),
        name="kv_proj",
    )(xn, w)


def _matmul_res_kernel(y_ref, w_ref, x_ref, g_ref, *out_refs, n_norm, keep_x, head_tiles):
    x = x_ref[...] + jnp.dot(y_ref[...], w_ref[...], preferred_element_type=F32)
    ms = jnp.mean(x * x, axis=-1, keepdims=True)
    xh = x * lax.rsqrt(ms + EPS)
    if head_tiles is not None:
        head_ref, tail_ref = out_refs
        i = pl.program_id(0)

        @pl.when(i < head_tiles)
        def _():
            head_ref[...] = xh * g_ref[0]

        @pl.when(i >= head_tiles)
        def _():
            tail_ref[...] = xh * g_ref[0]
        return
    pos = 0
    if keep_x:
        out_refs[0][...] = x
        pos = 1
    for t in range(n_norm):
        out_refs[pos + t][...] = (xh * g_ref[t]).astype(out_refs[pos + t].dtype)


def matmul_residual(y, w_all, layer, x, gains, norm_dtype, keep_x=True, split_rows=None):
    n, e = y.shape
    d = w_all.shape[2]
    gs = jnp.stack([g.reshape(1, d) for g in gains])
    if split_rows is None:
        tm = _row_tile(n, RES_ROWS)
        head_tiles = None
        row_spec = pl.BlockSpec((tm, d), lambda i: (i, 0))
        out_shapes = ([jax.ShapeDtypeStruct((n, d), F32)] if keep_x else []) + \
            [jax.ShapeDtypeStruct((n, d), norm_dtype) for _ in gains]
        out_specs = [row_spec] * len(out_shapes)
    else:
        assert len(gains) == 1 and not keep_x and norm_dtype == F32
        tm = _row_tile(math.gcd(split_rows, n - split_rows), RES_ROWS)
        head_tiles = split_rows // tm
        row_spec = pl.BlockSpec((tm, d), lambda i: (i, 0))
        out_shapes = [jax.ShapeDtypeStruct((split_rows, d), F32), jax.ShapeDtypeStruct((n - split_rows, d), F32)]
        out_specs = [pl.BlockSpec((tm, d), lambda i: (jnp.minimum(i, head_tiles - 1), 0)),
                     pl.BlockSpec((tm, d), lambda i: (jnp.maximum(i - head_tiles, 0), 0))]
    kern = functools.partial(_matmul_res_kernel, n_norm=len(gains), keep_x=keep_x, head_tiles=head_tiles)
    return pl.pallas_call(
        kern,
        grid=(n // tm,),
        in_specs=[
            pl.BlockSpec((tm, e), lambda i: (i, 0)),
            pl.BlockSpec((None, e, d), lambda i: (layer, 0, 0), pipeline_mode=pl.Buffered(1)),
            row_spec,
            pl.BlockSpec((len(gains), 1, d), lambda i: (0, 0, 0)),
        ],
        out_specs=out_specs,
        out_shape=out_shapes,
        compiler_params=_params("arbitrary"),
        name="matmul_residual",
    )(y, w_all, x, gs)


def _hgrn_kernel(*refs, chunk, n_chunks, hb, per_batch, zero_init, n_alias):
    refs = list(refs)
    zq_ref, zf_ref, zi_ref, zg_ref, on_ref = refs[:5]
    pos = 5
    s0_ref = None
    if not zero_init:
        s0_ref = refs[pos]
        pos += 1
    pos += n_alias
    y_ref, sfin_ref, s_ref, cum_sc, oint_sc, ystage_sc, flag_ref = refs[pos:]
    c = chunk
    t = pl.program_id(2)
    aligned = c % BF16_SUBLANES == 0

    @pl.when(t == 0)
    def _():
        if zero_init:
            s_ref[...] = jnp.zeros_like(s_ref)
        else:
            s_ref[...] = s0_ref[...]

    on_g = on_ref[...]
    if aligned:
        zi_all = zg_all = None
    else:
        zi_all = zi_ref[...].astype(F32)
        zg_all = zg_ref[...].astype(F32)

    ri = lax.broadcasted_iota(jnp.int32, (c, c), 0)
    ci_ = lax.broadcasted_iota(jnp.int32, (c, c), 1)
    tril = (ci_ <= ri).astype(BF16)
    row_c = lax.broadcasted_iota(jnp.int32, (c, LANES), 0)
    col_c = lax.broadcasted_iota(jnp.int32, (c, LANES), 1)
    causal = col_c <= row_c
    zpad_k = jnp.zeros((LANES - c, DK_A), F32)
    zpad_x = jnp.zeros((LANES - c - 8, DK_A), F32)

    def values(rows, vs_):
        return zi_ref[rows, vs_].astype(F32) if aligned else zi_all[rows, vs_]

    def values_padded(rows, vs_):
        if aligned:
            return jnp.concatenate([zi_ref[rows, vs_], jnp.zeros((LANES - c, DV_A), BF16)], axis=0)
        return jnp.concatenate([zi_all[rows, vs_], jnp.zeros((LANES - c, DV_A), F32)], axis=0).astype(BF16)

    def emit_y(o, rows, vs_):
        ms = jnp.mean(o * o, axis=-1, keepdims=True)
        o = o * lax.rsqrt(ms + EPS) * on_g
        if aligned:
            y_ref[rows, vs_] = (o * zg_ref[rows, vs_].astype(F32)).astype(y_ref.dtype)
        else:
            ystage_sc[rows, vs_] = o * zg_all[rows, vs_]

    for ci in range(n_chunks):
        rows = slice(ci * c, (ci + 1) * c)
        si = ci if per_batch else 0
        logf = zf_ref[rows, :]
        h1 = logf.astype(BF16)
        h2 = (logf - h1.astype(F32)).astype(BF16)
        cum = jnp.dot(tril, h1, preferred_element_type=F32) + jnp.dot(tril, h2, preferred_element_type=F32)
        last = cum[c - 1:c, :]
        mid = cum[c // 2 - 1:c // 2, :]
        dev = jnp.max(jnp.maximum(cum[0:1, :] - mid, mid - last))
        flag_ref[ci] = (dev <= FACTOR_SAFE).astype(jnp.int32)
        cum_sc[rows, :] = cum

        for h in range(hb):
            hs = slice(h * DK_A, (h + 1) * DK_A)
            vs_ = slice(h * DV_A, (h + 1) * DV_A)
            q_h = zq_ref[rows, hs]
            k_h = 1.0 - jnp.exp(zf_ref[rows, hs])
            cum_h = cum_sc[rows, hs]
            v_pad = values_padded(rows, vs_)
            d_h = cum_h - mid[:, hs]
            qm = (q_h * jnp.exp(d_h)).astype(BF16)
            km = jnp.concatenate([k_h * jnp.exp(-d_h), zpad_k], axis=0).astype(BF16)
            att = lax.dot_general(qm, km, (((1,), (1,)), ((), ())), preferred_element_type=F32)
            att = jnp.where(causal, att, 0.0).astype(BF16)
            o_intra = jnp.dot(att, v_pad, preferred_element_type=F32)
            s_prev = s_ref[si, h]
            qa = (q_h * jnp.exp(cum_h)).astype(BF16)
            o_inter = jnp.dot(qa, s_prev.astype(BF16), preferred_element_type=F32)
            oint_sc[rows, vs_] = o_inter
            kd = k_h * jnp.exp(last[:, hs] - cum_h)
            el8 = jnp.broadcast_to(jnp.exp(last[:, hs]), (8, DK_A))
            x_t = jnp.concatenate([kd, el8, zpad_x], axis=0).T
            s_ref[si, h] = x_t[:, c:c + 1] * s_prev + jnp.dot(x_t.astype(BF16), v_pad, preferred_element_type=F32)
            emit_y(o_inter + o_intra, rows, vs_)

    for ci in range(n_chunks):
        rows = slice(ci * c, (ci + 1) * c)

        @pl.when(flag_ref[ci] == 0)
        def _(rows=rows):
            row_v = lax.broadcasted_iota(jnp.int32, (c, 1), 0)

            def row_of(a, s):
                return jnp.sum(jnp.where(row_v == s, a, 0.0), axis=0, keepdims=True)

            for h in range(hb):
                hs = slice(h * DK_A, (h + 1) * DK_A)
                vs_ = slice(h * DV_A, (h + 1) * DV_A)
                q_h = zq_ref[rows, hs]
                k_h = 1.0 - jnp.exp(zf_ref[rows, hs])
                cum_h = cum_sc[rows, hs]
                v_h = values(rows, vs_)

                def body(s, acc_o, q_h=q_h, k_h=k_h, cum_h=cum_h, v_h=v_h):
                    dec = jnp.exp(jnp.minimum(cum_h - row_of(cum_h, s), 0.0))
                    w = jnp.sum(q_h * row_of(k_h, s) * dec, axis=-1, keepdims=True)
                    w = jnp.where(row_v >= s, w, 0.0)
                    return acc_o + w * row_of(v_h, s)

                o_intra = lax.fori_loop(0, c, body, jnp.zeros((c, DV_A), F32))
                emit_y(oint_sc[rows, vs_] + o_intra, rows, vs_)

    if not aligned:
        y_ref[...] = ystage_sc[...].astype(y_ref.dtype)

    @pl.when(t == pl.num_programs(2) - 1)
    def _():
        sfin_ref[...] = s_ref[...]


def hgrn_scan(zqf, zig, onorm_a, s0_all, y_prev, states_prev, *, layer, n_batch, rows_per_batch, row0, chunk,
              n_chunks, hb, per_batch):
    n = zqf.shape[0]
    tb = chunk * n_chunks
    ns = n_chunks if per_batch else 1
    nt = 1 if per_batch else rows_per_batch // tb
    nbg = n_batch // ns
    ng = H_A // hb
    wk, wv = hb * DK_A, hb * DV_A
    f_off = (H_A * DK_A) // wk
    g_off = E_WIDTH // wv
    rb0 = row0 // tb
    row = lambda b, g, t: rb0 + b * nt + t
    in_specs = [
        pl.BlockSpec((tb, wk), lambda b, g, t: (row(b, g, t), g)),
        pl.BlockSpec((tb, wk), lambda b, g, t: (row(b, g, t), f_off + g)),
        pl.BlockSpec((tb, wv), lambda b, g, t: (row(b, g, t), g)),
        pl.BlockSpec((tb, wv), lambda b, g, t: (row(b, g, t), g_off + g)),
        pl.BlockSpec((None, 1, DV_A), lambda b, g, t: (layer, 0, 0)),
    ]
    args = [zqf, zqf, zig, zig, onorm_a.reshape(N_A, 1, DV_A)]
    state_spec = pl.BlockSpec((None, ns, hb, DK_A, DV_A), lambda b, g, t: (layer, b, g, 0, 0))
    if s0_all is not None:
        in_specs.append(state_spec)
        args.append(s0_all)
    aliases = {}
    for out_idx, prev in ((0, y_prev), (1, states_prev)):
        if prev is not None:
            aliases[len(args)] = out_idx
            in_specs.append(pl.BlockSpec(memory_space=pl.ANY))
            args.append(prev)
    kern = functools.partial(_hgrn_kernel, chunk=chunk, n_chunks=n_chunks, hb=hb, per_batch=per_batch,
                             zero_init=s0_all is None, n_alias=len(aliases))
    return pl.pallas_call(
        kern,
        grid=(nbg, ng, nt),
        in_specs=in_specs,
        out_specs=[
            pl.BlockSpec((tb, wv), lambda b, g, t: (row(b, g, t), g)),
            state_spec,
        ],
        out_shape=[
            jax.ShapeDtypeStruct((n, E_WIDTH), BF16),
            jax.ShapeDtypeStruct((N_A, n_batch, H_A, DK_A, DV_A), F32),
        ],
        scratch_shapes=[
            pltpu.VMEM((ns, hb, DK_A, DV_A), F32),
            pltpu.VMEM((tb, wk), F32),
            pltpu.VMEM((tb, wv), F32),
            pltpu.VMEM((tb, wv), F32),
            pltpu.SMEM((n_chunks,), jnp.int32),
        ],
        input_output_aliases=aliases,
        compiler_params=_params("arbitrary", "arbitrary", "arbitrary"),
        name=f"hgrn_scan_c{chunk}",
    )(*args)


def _swa_prompt_kernel(sink_ref, q_ref, g_ref, kc_ref, kp_ref, vc_ref, vp_ref, y_ref, d_sc, s_sc, p_sc):
    blk = pl.program_id(1)
    kg = pl.program_id(2)
    w = WINDOW
    npair = G_B // 2

    @pl.when(kg == 0)
    def _():
        row = lax.broadcasted_iota(jnp.int32, (w, 2 * w), 0)
        col = lax.broadcasted_iota(jnp.int32, (w, 2 * w), 1)
        dist = w + row - col
        key_pos = (blk - 1) * w + col
        valid = (dist >= 0) & (dist < w) & (key_pos >= PAD)
        d_sc[...] = jnp.where(valid, dist.astype(F32), -NEG_BIG)

    lane = lax.broadcasted_iota(jnp.int32, (w, LANES), 1)
    lane_k = lax.broadcasted_iota(jnp.int32, (2 * w, LANES), 1)
    row_k = lax.broadcasted_iota(jnp.int32, (2 * w, LANES), 0)
    for kk in range(KH_STEP):
        kh = kg * KH_STEP + kk
        kdup = jnp.concatenate([kp_ref[kk], kc_ref[kk]], axis=0)
        k_par = [jnp.where(lane_k < HD_B, kdup, 0.0).astype(BF16), jnp.where(lane_k < HD_B, 0.0, kdup).astype(BF16)]
        vraw = jnp.concatenate([vp_ref[kk], vc_ref[kk]], axis=0)
        vext = jnp.where(row_k == 0, jnp.where(lane_k < HD_B, 0.0, vraw), vraw).astype(BF16)
        qs = jnp.concatenate(
            [q_ref[:, (kk * npair + j) * LANES:(kk * npair + j + 1) * LANES] for j in range(npair)],
            axis=0).astype(BF16)
        for par in range(2):
            s_sc[kk, par] = lax.dot_general(qs, k_par[par], (((1,), (1,)), ((), ())),
                                            preferred_element_type=F32)
        for j in range(npair):
            rows = slice(j * w, (j + 1) * w)
            for par in range(2):
                head = kh * G_B + 2 * j + par
                slope = jnp.exp(-LN2 * 8.0 * (head + 1).astype(F32) / H_B)
                sink = sink_ref[head]
                s_lo = s_sc[kk, par, rows, :w] - slope * d_sc[:, :w]
                s_lo = jnp.where(lane == 0, sink, s_lo)
                s_hi = s_sc[kk, par, rows, w:] - slope * d_sc[:, w:]
                m = jnp.max(jnp.maximum(s_lo, s_hi), axis=-1, keepdims=True)
                p_sc[kk, par, rows, :w] = jnp.exp(s_lo - m).astype(BF16)
                p_sc[kk, par, rows, w:] = jnp.exp(s_hi - m).astype(BF16)
        o_par = [jnp.dot(p_sc[kk, par], vext, preferred_element_type=F32) for par in range(2)]
        for j in range(npair):
            o_even = o_par[0][j * w:(j + 1) * w, :]
            o_odd = o_par[1][j * w:(j + 1) * w, :]
            num = jnp.where(lane < HD_B, o_even, pltpu.roll(o_odd, HD_B, 1))
            den = jnp.where(lane < HD_B, pltpu.roll(o_even, HD_B, 1), o_odd)
            cs = slice((kk * npair + j) * LANES, (kk * npair + j + 1) * LANES)
            y_ref[:, cs] = (num / den * g_ref[:, cs].astype(F32)).astype(BF16)


def swa_prompt(zb, kv_hm, sink, n_batch, blocks_per_batch):
    n = zb.shape[0]
    w = WINDOW
    gw = KH_STEP * G_B * HD_B
    nb = blocks_per_batch
    nkg = KV_B // KH_STEP
    cur = lambda b, i, kg: b * nb + i
    prev = lambda b, i, kg: b * nb + jnp.maximum(i - 1, 0)
    kv_block = (KH_STEP, w, 2 * HD_B)
    return pl.pallas_call(
        _swa_prompt_kernel,
        grid=(n_batch, nb, nkg),
        in_specs=[
            pl.BlockSpec(memory_space=pltpu.SMEM),
            pl.BlockSpec((w, gw), lambda b, i, kg: (cur(b, i, kg), kg)),
            pl.BlockSpec((w, gw), lambda b, i, kg: (cur(b, i, kg), nkg + kg)),
            pl.BlockSpec(kv_block, lambda b, i, kg: (kg, cur(b, i, kg), 0)),
            pl.BlockSpec(kv_block, lambda b, i, kg: (kg, prev(b, i, kg), 0)),
            pl.BlockSpec(kv_block, lambda b, i, kg: (nkg + kg, cur(b, i, kg), 0)),
            pl.BlockSpec(kv_block, lambda b, i, kg: (nkg + kg, prev(b, i, kg), 0)),
        ],
        out_specs=pl.BlockSpec((w, gw), lambda b, i, kg: (cur(b, i, kg), kg)),
        out_shape=jax.ShapeDtypeStruct((n, E_WIDTH), BF16),
        scratch_shapes=[
            pltpu.VMEM((w, 2 * w), F32),
            pltpu.VMEM((KH_STEP, 2, G_B // 2 * w, 2 * w), F32),
            pltpu.VMEM((KH_STEP, 2, G_B // 2 * w, 2 * w), BF16),
        ],
        compiler_params=_params("arbitrary", "arbitrary", "arbitrary"),
        name="swa_prompt",
    )(sink, zb, zb, kv_hm, kv_hm, kv_hm, kv_hm)


def _swa_sample_kernel(sink_ref, q_ref, g_ref, kn_ref, vn_ref, ck_ref, cv_ref, yprev_ref, y_ref, *, t_new, nbs):
    del yprev_ref
    wb = ck_ref.shape[1]
    nk = wb + t_new
    nr = G_B * t_new
    row = lax.broadcasted_iota(jnp.int32, (nr, nk), 0)
    col = lax.broadcasted_iota(jnp.int32, (nr, nk), 1)
    tok = row % t_new
    grp = row // t_new
    dist = wb + tok - col
    valid = (dist >= 0) & (dist < WINDOW)
    distf = dist.astype(F32)
    grp1 = lax.broadcasted_iota(jnp.int32, (nr, 1), 0) // t_new
    q_all = q_ref[...].astype(F32)
    g_all = g_ref[...].astype(F32)
    for kh in range(KV_B):
        ks = slice(kh * HD_B, (kh + 1) * HD_B)
        slope = jnp.exp(-LN2 * 8.0 * (kh * G_B + grp + 1).astype(F32) / H_B)
        sink = jnp.zeros((nr, 1), F32)
        for g in range(G_B):
            sink = jnp.where(grp1 == g, sink_ref[kh * G_B + g], sink)
        outs = []
        for bi in range(nbs):
            rb = slice(bi * t_new, (bi + 1) * t_new)
            kf = jnp.concatenate([ck_ref[bi, :, ks], kn_ref[kh][rb, :HD_B]], axis=0).astype(BF16)
            vf = jnp.concatenate([cv_ref[bi, :, ks], vn_ref[kh][rb, :HD_B]], axis=0).astype(BF16)
            qs = jnp.concatenate(
                [q_all[rb, (kh * G_B + g) * HD_B:(kh * G_B + g + 1) * HD_B] for g in range(G_B)], axis=0)
            s = lax.dot_general(qs.astype(BF16), kf, (((1,), (1,)), ((), ())), preferred_element_type=F32)
            s = s - slope * distf
            s = jnp.where(valid, s, NEG_BIG)
            m = jnp.maximum(jnp.max(s, axis=-1, keepdims=True), sink)
            p = jnp.where(valid, jnp.exp(s - m), 0.0)
            den = jnp.sum(p, axis=-1, keepdims=True) + jnp.exp(sink - m)
            outs.append(jnp.dot(p.astype(BF16), vf, preferred_element_type=F32) / den)
        for g in range(G_B):
            hsl = slice((kh * G_B + g) * HD_B, (kh * G_B + g + 1) * HD_B)
            o = jnp.concatenate([outs[bi][g * t_new:(g + 1) * t_new, :] for bi in range(nbs)], axis=0)
            y_ref[:, hsl] = (o * g_all[:, hsl]).astype(y_ref.dtype)


def swa_sample(zb, kv_hm, sink, y_prev, cache_k, cache_v, *, row0, n_batch, t_new):
    n = zb.shape[0]
    wb = cache_k.shape[1]
    nbs = SAMPLE_BATCHES
    tb = nbs * t_new
    rb0 = row0 // tb
    kern = functools.partial(_swa_sample_kernel, t_new=t_new, nbs=nbs)
    return pl.pallas_call(
        kern,
        grid=(n_batch // nbs,),
        in_specs=[
            pl.BlockSpec(memory_space=pltpu.SMEM),
            pl.BlockSpec((tb, E_WIDTH), lambda b: (rb0 + b, 0)),
            pl.BlockSpec((tb, E_WIDTH), lambda b: (rb0 + b, 1)),
            pl.BlockSpec((KV_B, tb, 2 * HD_B), lambda b: (0, rb0 + b, 0)),
            pl.BlockSpec((KV_B, tb, 2 * HD_B), lambda b: (1, rb0 + b, 0)),
            pl.BlockSpec((nbs, wb, KV_B * HD_B), lambda b: (b, 0, 0)),
            pl.BlockSpec((nbs, wb, KV_B * HD_B), lambda b: (b, 0, 0)),
            pl.BlockSpec(memory_space=pl.ANY),
        ],
        out_specs=pl.BlockSpec((tb, E_WIDTH), lambda b: (rb0 + b, 0)),
        out_shape=jax.ShapeDtypeStruct((n, E_WIDTH), BF16),
        input_output_aliases={7: 0},
        compiler_params=_params("arbitrary"),
        name="swa_sample",
    )(sink, zb, zb, kv_hm, kv_hm, cache_k, cache_v, y_prev)


def kernel(x_prompt, x_sample, state_hgrn, cache_k, cache_v, meta_tokens, norm_a, w_in_a, lb_a, onorm_a,
           w_out_a, norm_kv, w_kv, norm_b, w_in_b, sink_b, w_out_b, norm_f):
    bsz, seq, d = x_prompt.shape
    dec_b, dec_t, _ = x_sample.shape
    wb = cache_k.shape[1]
    w_out_a16 = w_out_a.astype(BF16)
    w_out_b16 = w_out_b.astype(BF16)

    rows_p = PAD + N_META + seq
    nb_p = rows_p // ROW_BLOCK
    n_p = bsz * rows_p
    n_s = dec_b * dec_t
    fd = 2 * H_A * DK_A
    s0_s = state_hgrn.astype(F32)
    ck = cache_k.reshape(dec_b, wb, KV_B * HD_B)
    cv = cache_v.reshape(dec_b, wb, KV_B * HD_B)

    x, xn = assemble(x_prompt, meta_tokens, x_sample.reshape(n_s, d), norm_a, 0)
    sp = ss = None
    for layer in range(N_A):
        zqf = proj(xn, w_in_a, layer, 0, ((fd // 2, "silu_dk"), (fd // 2, "logf")), F32, lb_a=lb_a)
        zig = proj(xn, w_in_a, layer, fd, ((E_WIDTH, "id"), (E_WIDTH, "silu")), BF16)
        y, sp = hgrn_scan(zqf, zig, onorm_a, None, None, sp, layer=layer, n_batch=bsz, rows_per_batch=rows_p,
                          row0=0, chunk=64, n_chunks=ROW_BLOCK // 64, hb=16, per_batch=False)
        y, ss = hgrn_scan(zqf, zig, onorm_a, s0_s, y, ss, layer=layer, n_batch=dec_b, rows_per_batch=dec_t,
                          row0=n_p, chunk=dec_t, n_chunks=SAMPLE_BATCHES, hb=8, per_batch=True)
        if layer + 1 < N_A:
            x, xn = matmul_residual(y, w_out_a16, layer, x, (norm_a[layer + 1],), BF16)
        else:
            x, xn, xn_kv = matmul_residual(y, w_out_a16, layer, x, (norm_b[0], norm_kv), BF16)
    kv_hm = kv_proj(xn_kv, w_kv)
    for layer in range(N_B):
        zb = proj(xn, w_in_b, layer, 0, ((E_WIDTH, "scale_hd"), (E_WIDTH, "silu")), BF16)
        y = swa_prompt(zb, kv_hm, sink_b[layer], bsz, nb_p)
        y = swa_sample(zb, kv_hm, sink_b[layer], y, ck, cv, row0=n_p, n_batch=dec_b, t_new=dec_t)
        if layer + 1 < N_B:
            x, xn = matmul_residual(y, w_out_b16, layer, x, (norm_b[layer + 1],), BF16)
        else:
            out_p, out_s = matmul_residual(y, w_out_b16, layer, x, (norm_f,), F32, keep_x=False, split_rows=n_p)

    y_prompt = out_p.reshape(bsz, rows_p, d)[:, PAD + N_META:]
    y_sample = out_s.reshape(dec_b, dec_t, d)
    kvp = jnp.stack([kv_hm[:, (b + 1) * rows_p - wb:(b + 1) * rows_p, :HD_B] for b in range(bsz)], axis=1)
    kvp = kvp.reshape(2, KV_B, bsz, wb, HD_B)
    cache_k_prompt = jnp.transpose(kvp[0], (1, 2, 0, 3)).astype(cache_k.dtype)
    cache_v_prompt = jnp.transpose(kvp[1], (1, 2, 0, 3)).astype(cache_v.dtype)
    kvs = kv_hm[:, n_p:, :HD_B].reshape(2, KV_B, dec_b, dec_t, HD_B)
    k_new = jnp.transpose(kvs[0], (1, 2, 0, 3)).astype(cache_k.dtype)
    v_new = jnp.transpose(kvs[1], (1, 2, 0, 3)).astype(cache_v.dtype)
    cache_k_sample = jnp.concatenate([cache_k, k_new], axis=1)[:, -wb:]
    cache_v_sample = jnp.concatenate([cache_v, v_new], axis=1)[:, -wb:]

    return (y_prompt, y_sample, sp.astype(state_hgrn.dtype), cache_k_prompt, cache_v_prompt,
            ss.astype(state_hgrn.dtype), cache_k_sample, cache_v_sample)
```

```python
import functools
import math

import jax
import jax.numpy as jnp
from jax import lax
from jax.experimental import pallas as pl
from jax.experimental.pallas import tpu as pltpu

D_MODEL = 2048
E_WIDTH = 2 * D_MODEL
N_A = 2
N_B = 2
DK_A = 128
H_A = D_MODEL // DK_A
DV_A = E_WIDTH // H_A
HD_B = 64
H_B = E_WIDTH // HD_B
KV_B = H_B // 8
G_B = H_B // KV_B
WINDOW = 128
N_META = 16
EPS = 1e-6
NEG_BIG = -1e30
TINY = 1e-30

LANES = 128
BF16_SUBLANES = 16
ROW_BLOCK = 128
PAD = ROW_BLOCK - N_META
FACTOR_SAFE = 60.0
LN2 = 0.6931471805599453
VMEM_LIMIT = 56 * 1024 * 1024
KH_STEP = 8
SAMPLE_BATCHES = BF16_SUBLANES // 8
PROJ_ROWS = 1120
RES_ROWS = 448
PROJ_COLS = 1024
PROJ_SUBTILES = 5

F32 = jnp.float32
BF16 = jnp.bfloat16


def _sigmoid(x):
    return 1.0 / (1.0 + jnp.exp(-x))


def _silu(x):
    return x * _sigmoid(x)


def _params(*sem):
    return pltpu.CompilerParams(dimension_semantics=sem, vmem_limit_bytes=VMEM_LIMIT)


def _row_tile(n, cap):
    for t in range(min(cap, n) // BF16_SUBLANES * BF16_SUBLANES, 0, -BF16_SUBLANES):
        if n % t == 0:
            return t
    raise ValueError(f"no row tile for {n}")


def _assemble_kernel(xp_ref, meta_ref, xs_ref, g_ref, x_ref, xn_ref, *, n_prompt_blocks, blocks_per_batch):
    r = pl.program_id(0)
    is_prompt = r < n_prompt_blocks
    first = (r % blocks_per_batch) == 0

    def emit(x):
        x_ref[...] = x
        ms = jnp.mean(x * x, axis=-1, keepdims=True)
        xn_ref[...] = (x * lax.rsqrt(ms + EPS) * g_ref[...]).astype(xn_ref.dtype)

    @pl.when(is_prompt & first)
    def _():
        emit(jnp.concatenate([jnp.zeros((PAD, x_ref.shape[1]), F32), meta_ref[...].astype(F32)], axis=0))

    @pl.when(is_prompt & jnp.logical_not(first))
    def _():
        emit(xp_ref[...].astype(F32))

    @pl.when(jnp.logical_not(is_prompt))
    def _():
        emit(xs_ref[...].astype(F32))


def assemble(x_prompt, meta_tokens, x_sample2d, g_all, layer):
    bsz, seq, d = x_prompt.shape
    n_s = x_sample2d.shape[0]
    assert seq % ROW_BLOCK == 0 and n_s % ROW_BLOCK == 0, (seq, n_s)
    nb = 1 + seq // ROW_BLOCK
    npb = bsz * nb
    nsb = n_s // ROW_BLOCK
    n = (npb + nsb) * ROW_BLOCK
    kern = functools.partial(_assemble_kernel, n_prompt_blocks=npb, blocks_per_batch=nb)
    row_spec = pl.BlockSpec((ROW_BLOCK, d), lambda r: (r, 0))
    return pl.pallas_call(
        kern,
        grid=(npb + nsb,),
        in_specs=[
            pl.BlockSpec((None, ROW_BLOCK, d),
                         lambda r: (jnp.minimum(r // nb, bsz - 1), jnp.maximum(r % nb - 1, 0), 0)),
            pl.BlockSpec((N_META, d), lambda r: (0, 0)),
            pl.BlockSpec((ROW_BLOCK, d), lambda r: (jnp.clip(r - npb, 0, nsb - 1), 0)),
            pl.BlockSpec((None, 1, d), lambda r: (layer, 0, 0)),
        ],
        out_specs=[row_spec, row_spec],
        out_shape=[jax.ShapeDtypeStruct((n, d), F32), jax.ShapeDtypeStruct((n, d), BF16)],
        compiler_params=_params("arbitrary"),
        name="assemble",
    )(x_prompt, meta_tokens, x_sample2d, g_all.reshape(g_all.shape[0], 1, d))


def _log_forget(f, lbp, layer):
    e = jnp.exp(lbp - jnp.max(lbp, axis=0, keepdims=True))
    p = e / jnp.sum(e, axis=0, keepdims=True)
    acc = p[0:1]
    for j in range(1, layer + 1):
        acc = acc + p[j:j + 1]
    lb = acc - p[0:1]
    ef = jnp.exp(-jnp.abs(f))
    r = 1.0 / (1.0 + ef)
    sig = jnp.where(f >= 0, 1.0, ef) * r
    pos = lb > 0
    arg = jnp.where(pos, jnp.maximum(lb, TINY) + (1.0 - lb) * sig, r)
    off = jnp.where(pos, 0.0, jnp.log1p(-lb) + jnp.minimum(f, 0.0))
    return jnp.log(arg) + off


def _proj_kernel(*refs, segments, layer, with_cast):
    refs = list(refs)
    x_ref, w_ref = refs[:2]
    pos = 2
    lb_ref = cast_src_ref = cast_dst_ref = None
    if any(mode == "logf" for _, _, mode in segments):
        lb_ref = refs[pos]
        pos += 1
    if with_cast:
        cast_src_ref = refs[pos]
        pos += 1
    o_ref = refs[pos]
    pos += 1
    if with_cast:
        cast_dst_ref = refs[pos]
        pos += 1
    wb_ref = refs[pos]
    j = pl.program_id(0)

    if with_cast:
        cast_dst_ref[...] = cast_src_ref[...].astype(BF16)

    @pl.when(pl.program_id(1) == 0)
    def _():
        wb_ref[...] = w_ref[...].astype(BF16)

    for j0, j1, mode in segments:
        @pl.when((j >= j0) & (j < j1))
        def _(mode=mode):
            tm = x_ref.shape[0]
            nsub = PROJ_SUBTILES * max(tm // PROJ_ROWS, 1)
            sub = tm // nsub if tm % (nsub * BF16_SUBLANES) == 0 else tm
            for r0 in range(0, tm, sub):
                acc = jnp.dot(x_ref[r0:r0 + sub, :], wb_ref[...], preferred_element_type=F32)
                if mode == "silu":
                    acc = _silu(acc)
                elif mode == "silu_dk":
                    acc = _silu(acc) * (DK_A ** -0.5)
                elif mode == "scale_hd":
                    acc = acc * (HD_B ** -0.5)
                elif mode == "logf":
                    acc = _log_forget(acc, lb_ref[...], layer)
                else:
                    assert mode == "id", mode
                o_ref[r0:r0 + sub, :] = acc.astype(o_ref.dtype)


def proj(xn, w_all, layer, col0, segments, out_dtype, lb_a=None, cast_all=None):
    n, d = xn.shape
    tn = PROJ_COLS
    tm = _row_tile(n, PROJ_ROWS * (4 // jnp.dtype(out_dtype).itemsize))
    jb = col0 // tn
    bounds, j0 = [], 0
    for ncols_seg, mode in segments:
        bounds.append((j0, j0 + ncols_seg // tn, mode))
        j0 += ncols_seg // tn
    ncols = j0 * tn
    in_specs = [
        pl.BlockSpec((tm, d), lambda j, i: (i, 0)),
        pl.BlockSpec((None, d, tn), lambda j, i: (layer, 0, jb + j)),
    ]
    args = [xn, w_all]
    for ja, jz, mode in bounds:
        if mode == "logf":
            in_specs.append(pl.BlockSpec((lb_a.shape[0], tn),
                                         lambda j, i, ja=ja, jz=jz: (0, jnp.clip(j - ja, 0, jz - ja - 1))))
            args.append(lb_a)
    nj, ni = ncols // tn, n // tm
    out_specs = [pl.BlockSpec((tm, tn), lambda j, i: (i, j))]
    out_shape = [jax.ShapeDtypeStruct((n, ncols), out_dtype)]
    if cast_all is not None:
        _, ce, cd = cast_all.shape
        slab = ce // (nj * ni)
        assert slab * nj * ni == ce and slab % BF16_SUBLANES == 0, (ce, nj, ni)
        in_specs.append(pl.BlockSpec((None, slab, cd), lambda j, i: (layer, j * ni + i, 0)))
        args.append(cast_all)
        out_specs.append(pl.BlockSpec((slab, cd), lambda j, i: (j * ni + i, 0)))
        out_shape.append(jax.ShapeDtypeStruct((ce, cd), BF16))
    kern = functools.partial(_proj_kernel, segments=tuple(bounds), layer=layer, with_cast=cast_all is not None)
    res = pl.pallas_call(
        kern,
        grid=(nj, ni),
        in_specs=in_specs,
        out_specs=out_specs,
        out_shape=out_shape,
        scratch_shapes=[pltpu.VMEM((d, tn), BF16)],
        compiler_params=_params("arbitrary", "arbitrary"),
        name="proj",
    )(*args)
    return res if cast_all is not None else res[0]


def _kv_proj_kernel(x_ref, w_ref, o_ref, wb_ref):
    @pl.when(pl.program_id(0) == 0)
    def _():
        wb_ref[...] = w_ref[...].astype(BF16)

    kv = jnp.dot(x_ref[...], wb_ref[...], preferred_element_type=F32)
    ones = jnp.ones((kv.shape[0], HD_B), F32)
    for j in range(KV_B):
        k_j = kv[:, j * HD_B:(j + 1) * HD_B]
        o_ref[j] = jnp.concatenate([k_j, k_j], axis=1)
        v_j = kv[:, (KV_B + j) * HD_B:(KV_B + j + 1) * HD_B]
        o_ref[KV_B + j] = jnp.concatenate([v_j, ones], axis=1)


def kv_proj(xn, w):
    n, d = xn.shape
    tm = _row_tile(n, RES_ROWS)
    return pl.pallas_call(
        _kv_proj_kernel,
        grid=(n // tm,),
        in_specs=[
            pl.BlockSpec((tm, d), lambda i: (i, 0)),
            pl.BlockSpec((d, 2 * KV_B * HD_B), lambda i: (0, 0), pipeline_mode=pl.Buffered(1)),
        ],
        out_specs=pl.BlockSpec((2 * KV_B, tm, 2 * HD_B), lambda i: (0, i, 0)),
        out_shape=jax.ShapeDtypeStruct((2 * KV_B, n, 2 * HD_B), F32),
        scratch_shapes=[pltpu.VMEM((d, 2 * KV_B * HD_B), BF16)],
        compiler_params=_params("arbitrary"),
        name="kv_proj",
    )(xn, w)


def _matmul_res_kernel(y_ref, w_ref, x_ref, g_ref, *out_refs, n_norm, keep_x, head_tiles):
    x = x_ref[...] + jnp.dot(y_ref[...], w_ref[...], preferred_element_type=F32)
    ms = jnp.mean(x * x, axis=-1, keepdims=True)
    xh = x * lax.rsqrt(ms + EPS)
    if head_tiles is not None:
        head_ref, tail_ref = out_refs
        i = pl.program_id(0)

        @pl.when(i < head_tiles)
        def _():
            head_ref[...] = xh * g_ref[0]

        @pl.when(i >= head_tiles)
        def _():
            tail_ref[...] = xh * g_ref[0]
        return
    pos = 0
    if keep_x:
        out_refs[0][...] = x
        pos = 1
    for t in range(n_norm):
        out_refs[pos + t][...] = (xh * g_ref[t]).astype(out_refs[pos + t].dtype)


def matmul_residual(y, w, x, gains, norm_dtype, keep_x=True, split_rows=None):
    n, e = y.shape
    d = w.shape[1]
    gs = jnp.stack([g.reshape(1, d) for g in gains])
    if split_rows is None:
        tm = _row_tile(n, RES_ROWS)
        head_tiles = None
        row_spec = pl.BlockSpec((tm, d), lambda i: (i, 0))
        out_shapes = ([jax.ShapeDtypeStruct((n, d), F32)] if keep_x else []) + \
            [jax.ShapeDtypeStruct((n, d), norm_dtype) for _ in gains]
        out_specs = [row_spec] * len(out_shapes)
    else:
        assert len(gains) == 1 and not keep_x and norm_dtype == F32
        tm = _row_tile(math.gcd(split_rows, n - split_rows), RES_ROWS)
        head_tiles = split_rows // tm
        row_spec = pl.BlockSpec((tm, d), lambda i: (i, 0))
        out_shapes = [jax.ShapeDtypeStruct((split_rows, d), F32), jax.ShapeDtypeStruct((n - split_rows, d), F32)]
        out_specs = [pl.BlockSpec((tm, d), lambda i: (jnp.minimum(i, head_tiles - 1), 0)),
                     pl.BlockSpec((tm, d), lambda i: (jnp.maximum(i - head_tiles, 0), 0))]
    kern = functools.partial(_matmul_res_kernel, n_norm=len(gains), keep_x=keep_x, head_tiles=head_tiles)
    return pl.pallas_call(
        kern,
        grid=(n // tm,),
        in_specs=[
            pl.BlockSpec((tm, e), lambda i: (i, 0)),
            pl.BlockSpec((e, d), lambda i: (0, 0), pipeline_mode=pl.Buffered(1)),
            row_spec,
            pl.BlockSpec((len(gains), 1, d), lambda i: (0, 0, 0)),
        ],
        out_specs=out_specs,
        out_shape=out_shapes,
        compiler_params=_params("arbitrary"),
        name="matmul_residual",
    )(y, w, x, gs)


def _hgrn_kernel(*refs, chunk, n_chunks, hb, per_batch, zero_init, n_alias):
    refs = list(refs)
    zq_ref, zf_ref, zi_ref, zg_ref, on_ref = refs[:5]
    pos = 5
    s0_ref = None
    if not zero_init:
        s0_ref = refs[pos]
        pos += 1
    pos += n_alias
    y_ref, sfin_ref, s_ref, cum_sc, oint_sc, ystage_sc, flag_ref = refs[pos:]
    c = chunk
    t = pl.program_id(2)
    aligned = c % BF16_SUBLANES == 0

    @pl.when(t == 0)
    def _():
        if zero_init:
            s_ref[...] = jnp.zeros_like(s_ref)
        else:
            s_ref[...] = s0_ref[...]

    on_g = on_ref[...]
    if aligned:
        zi_all = zg_all = None
    else:
        zi_all = zi_ref[...].astype(F32)
        zg_all = zg_ref[...].astype(F32)

    ri = lax.broadcasted_iota(jnp.int32, (c, c), 0)
    ci_ = lax.broadcasted_iota(jnp.int32, (c, c), 1)
    tril = (ci_ <= ri).astype(BF16)
    row_c = lax.broadcasted_iota(jnp.int32, (c, LANES), 0)
    col_c = lax.broadcasted_iota(jnp.int32, (c, LANES), 1)
    causal = col_c <= row_c
    zpad_k = jnp.zeros((LANES - c, DK_A), F32)
    zpad_x = jnp.zeros((LANES - c - 8, DK_A), F32)

    def values(rows, vs_):
        return zi_ref[rows, vs_].astype(F32) if aligned else zi_all[rows, vs_]

    def values_padded(rows, vs_):
        return jnp.concatenate([zi_all[rows, vs_], jnp.zeros((LANES - c, DV_A), F32)], axis=0).astype(BF16)

    def emit_y(o, rows, vs_):
        ms = jnp.mean(o * o, axis=-1, keepdims=True)
        o = o * lax.rsqrt(ms + EPS) * on_g
        if aligned:
            y_ref[rows, vs_] = (o * zg_ref[rows, vs_].astype(F32)).astype(y_ref.dtype)
        else:
            ystage_sc[rows, vs_] = o * zg_all[rows, vs_]

    for ci in range(n_chunks):
        rows = slice(ci * c, (ci + 1) * c)
        si = ci if per_batch else 0
        logf = zf_ref[rows, :]
        h1 = logf.astype(BF16)
        h2 = (logf - h1.astype(F32)).astype(BF16)
        cum = jnp.dot(tril, h1, preferred_element_type=F32) + jnp.dot(tril, h2, preferred_element_type=F32)
        last = cum[c - 1:c, :]
        mid = cum[c // 2 - 1:c // 2, :]
        dev = jnp.max(jnp.maximum(cum[0:1, :] - mid, mid - last))
        flag_ref[ci] = (dev <= FACTOR_SAFE).astype(jnp.int32)
        cum_sc[rows, :] = cum

        for h in range(hb):
            hs = slice(h * DK_A, (h + 1) * DK_A)
            vs_ = slice(h * DV_A, (h + 1) * DV_A)
            q_h = zq_ref[rows, hs]
            k_h = 1.0 - jnp.exp(zf_ref[rows, hs])
            cum_h = cum_sc[rows, hs]
            d_h = cum_h - mid[:, hs]
            qm = (q_h * jnp.exp(d_h)).astype(BF16)
            km = k_h * jnp.exp(-d_h)
            s_prev = s_ref[si, h]
            qa = (q_h * jnp.exp(cum_h)).astype(BF16)
            o_inter = jnp.dot(qa, s_prev.astype(BF16), preferred_element_type=F32)
            oint_sc[rows, vs_] = o_inter
            kd = k_h * jnp.exp(last[:, hs] - cum_h)
            el8 = jnp.broadcast_to(jnp.exp(last[:, hs]), (8, DK_A))
            x_t = jnp.concatenate([kd, el8, zpad_x], axis=0).T
            if aligned:
                att = lax.dot_general(qm, km.astype(BF16), (((1,), (1,)), ((), ())), preferred_element_type=F32)
                att = jnp.where(causal[:, :c], att, 0.0).astype(BF16)
                lhs = jnp.concatenate([att, x_t[:, :c].astype(BF16)], axis=0)
                res = jnp.dot(lhs, zi_ref[rows, vs_], preferred_element_type=F32)
                o_intra, ds = res[:c], res[c:]
            else:
                v_pad = values_padded(rows, vs_)
                km_pad = jnp.concatenate([km, zpad_k], axis=0).astype(BF16)
                att = lax.dot_general(qm, km_pad, (((1,), (1,)), ((), ())), preferred_element_type=F32)
                att = jnp.where(causal, att, 0.0).astype(BF16)
                o_intra = jnp.dot(att, v_pad, preferred_element_type=F32)
                ds = jnp.dot(x_t.astype(BF16), v_pad, preferred_element_type=F32)
            s_ref[si, h] = x_t[:, c:c + 1] * s_prev + ds
            emit_y(o_inter + o_intra, rows, vs_)

    for ci in range(n_chunks):
        rows = slice(ci * c, (ci + 1) * c)

        @pl.when(flag_ref[ci] == 0)
        def _(rows=rows):
            row_v = lax.broadcasted_iota(jnp.int32, (c, 1), 0)

            def row_of(a, s):
                return jnp.sum(jnp.where(row_v == s, a, 0.0), axis=0, keepdims=True)

            for h in range(hb):
                hs = slice(h * DK_A, (h + 1) * DK_A)
                vs_ = slice(h * DV_A, (h + 1) * DV_A)
                q_h = zq_ref[rows, hs]
                k_h = 1.0 - jnp.exp(zf_ref[rows, hs])
                cum_h = cum_sc[rows, hs]
                v_h = values(rows, vs_)

                def body(s, acc_o, q_h=q_h, k_h=k_h, cum_h=cum_h, v_h=v_h):
                    dec = jnp.exp(jnp.minimum(cum_h - row_of(cum_h, s), 0.0))
                    w = jnp.sum(q_h * row_of(k_h, s) * dec, axis=-1, keepdims=True)
                    w = jnp.where(row_v >= s, w, 0.0)
                    return acc_o + w * row_of(v_h, s)

                o_intra = lax.fori_loop(0, c, body, jnp.zeros((c, DV_A), F32))
                emit_y(oint_sc[rows, vs_] + o_intra, rows, vs_)

    if not aligned:
        y_ref[...] = ystage_sc[...].astype(y_ref.dtype)

    @pl.when(t == pl.num_programs(2) - 1)
    def _():
        sfin_ref[...] = s_ref[...]


def hgrn_scan(zqf, zig, onorm_a, s0_all, y_prev, states_prev, *, layer, n_batch, rows_per_batch, row0, chunk,
              n_chunks, hb, per_batch):
    n = zqf.shape[0]
    tb = chunk * n_chunks
    ns = n_chunks if per_batch else 1
    nt = 1 if per_batch else rows_per_batch // tb
    nbg = n_batch // ns
    ng = H_A // hb
    wk, wv = hb * DK_A, hb * DV_A
    f_off = (H_A * DK_A) // wk
    g_off = E_WIDTH // wv
    rb0 = row0 // tb
    row = lambda b, g, t: rb0 + b * nt + t
    in_specs = [
        pl.BlockSpec((tb, wk), lambda b, g, t: (row(b, g, t), g)),
        pl.BlockSpec((tb, wk), lambda b, g, t: (row(b, g, t), f_off + g)),
        pl.BlockSpec((tb, wv), lambda b, g, t: (row(b, g, t), g)),
        pl.BlockSpec((tb, wv), lambda b, g, t: (row(b, g, t), g_off + g)),
        pl.BlockSpec((None, 1, DV_A), lambda b, g, t: (layer, 0, 0)),
    ]
    args = [zqf, zqf, zig, zig, onorm_a.reshape(N_A, 1, DV_A)]
    state_spec = pl.BlockSpec((None, ns, hb, DK_A, DV_A), lambda b, g, t: (layer, b, g, 0, 0))
    if s0_all is not None:
        in_specs.append(state_spec)
        args.append(s0_all)
    aliases = {}
    for out_idx, prev in ((0, y_prev), (1, states_prev)):
        if prev is not None:
            aliases[len(args)] = out_idx
            in_specs.append(pl.BlockSpec(memory_space=pl.ANY))
            args.append(prev)
    kern = functools.partial(_hgrn_kernel, chunk=chunk, n_chunks=n_chunks, hb=hb, per_batch=per_batch,
                             zero_init=s0_all is None, n_alias=len(aliases))
    return pl.pallas_call(
        kern,
        grid=(nbg, ng, nt),
        in_specs=in_specs,
        out_specs=[
            pl.BlockSpec((tb, wv), lambda b, g, t: (row(b, g, t), g)),
            state_spec,
        ],
        out_shape=[
            jax.ShapeDtypeStruct((n, E_WIDTH), BF16),
            jax.ShapeDtypeStruct((N_A, n_batch, H_A, DK_A, DV_A), F32),
        ],
        scratch_shapes=[
            pltpu.VMEM((ns, hb, DK_A, DV_A), F32),
            pltpu.VMEM((tb, wk), F32),
            pltpu.VMEM((tb, wv), F32),
            pltpu.VMEM((tb, wv), F32),
            pltpu.SMEM((n_chunks,), jnp.int32),
        ],
        input_output_aliases=aliases,
        compiler_params=_params("arbitrary", "arbitrary", "arbitrary"),
        name=f"hgrn_scan_c{chunk}",
    )(*args)


def _swa_prompt_kernel(sink_ref, q_ref, g_ref, kc_ref, kp_ref, vc_ref, vp_ref, y_ref, d_sc, s_sc, p_sc):
    blk = pl.program_id(1)
    kg = pl.program_id(2)
    w = WINDOW
    npair = G_B // 2

    @pl.when(kg == 0)
    def _():
        row = lax.broadcasted_iota(jnp.int32, (w, 2 * w), 0)
        col = lax.broadcasted_iota(jnp.int32, (w, 2 * w), 1)
        dist = w + row - col
        key_pos = (blk - 1) * w + col
        valid = (dist >= 0) & (dist < w) & (key_pos >= PAD)
        d_sc[...] = jnp.where(valid, dist.astype(F32), -NEG_BIG)

    lane = lax.broadcasted_iota(jnp.int32, (w, LANES), 1)
    lane_k = lax.broadcasted_iota(jnp.int32, (2 * w, LANES), 1)
    row_k = lax.broadcasted_iota(jnp.int32, (2 * w, LANES), 0)
    for kk in range(KH_STEP):
        kh = kg * KH_STEP + kk
        kdup = jnp.concatenate([kp_ref[kk], kc_ref[kk]], axis=0)
        k_par = [jnp.where(lane_k < HD_B, kdup, 0.0).astype(BF16), jnp.where(lane_k < HD_B, 0.0, kdup).astype(BF16)]
        vraw = jnp.concatenate([vp_ref[kk], vc_ref[kk]], axis=0)
        vext = jnp.where(row_k == 0, jnp.where(lane_k < HD_B, 0.0, vraw), vraw).astype(BF16)
        qs = jnp.concatenate(
            [q_ref[:, (kk * npair + j) * LANES:(kk * npair + j + 1) * LANES] for j in range(npair)],
            axis=0).astype(BF16)
        for par in range(2):
            s_sc[kk, par] = lax.dot_general(qs, k_par[par], (((1,), (1,)), ((), ())),
                                            preferred_element_type=F32)
        for j in range(npair):
            rows = slice(j * w, (j + 1) * w)
            for par in range(2):
                head = kh * G_B + 2 * j + par
                slope = jnp.exp(-LN2 * 8.0 * (head + 1).astype(F32) / H_B)
                sink = sink_ref[head]
                s_lo = s_sc[kk, par, rows, :w] - slope * d_sc[:, :w]
                s_lo = jnp.where(lane == 0, sink, s_lo)
                s_hi = s_sc[kk, par, rows, w:] - slope * d_sc[:, w:]
                m = jnp.max(jnp.maximum(s_lo, s_hi), axis=-1, keepdims=True)
                p_sc[kk, par, rows, :w] = jnp.exp(s_lo - m).astype(BF16)
                p_sc[kk, par, rows, w:] = jnp.exp(s_hi - m).astype(BF16)
        o_par = [jnp.dot(p_sc[kk, par], vext, preferred_element_type=F32) for par in range(2)]
        for j in range(npair):
            o_even = o_par[0][j * w:(j + 1) * w, :]
            o_odd = o_par[1][j * w:(j + 1) * w, :]
            num = jnp.where(lane < HD_B, o_even, pltpu.roll(o_odd, HD_B, 1))
            den = jnp.where(lane < HD_B, pltpu.roll(o_even, HD_B, 1), o_odd)
            cs = slice((kk * npair + j) * LANES, (kk * npair + j + 1) * LANES)
            y_ref[:, cs] = (num / den * g_ref[:, cs].astype(F32)).astype(BF16)


def swa_prompt(zb, kv_hm, sink, n_batch, blocks_per_batch):
    n = zb.shape[0]
    w = WINDOW
    gw = KH_STEP * G_B * HD_B
    nb = blocks_per_batch
    nkg = KV_B // KH_STEP
    cur = lambda b, i, kg: b * nb + i
    prev = lambda b, i, kg: b * nb + jnp.maximum(i - 1, 0)
    kv_block = (KH_STEP, w, 2 * HD_B)
    return pl.pallas_call(
        _swa_prompt_kernel,
        grid=(n_batch, nb, nkg),
        in_specs=[
            pl.BlockSpec(memory_space=pltpu.SMEM),
            pl.BlockSpec((w, gw), lambda b, i, kg: (cur(b, i, kg), kg)),
            pl.BlockSpec((w, gw), lambda b, i, kg: (cur(b, i, kg), nkg + kg)),
            pl.BlockSpec(kv_block, lambda b, i, kg: (kg, cur(b, i, kg), 0)),
            pl.BlockSpec(kv_block, lambda b, i, kg: (kg, prev(b, i, kg), 0)),
            pl.BlockSpec(kv_block, lambda b, i, kg: (nkg + kg, cur(b, i, kg), 0)),
            pl.BlockSpec(kv_block, lambda b, i, kg: (nkg + kg, prev(b, i, kg), 0)),
        ],
        out_specs=pl.BlockSpec((w, gw), lambda b, i, kg: (cur(b, i, kg), kg)),
        out_shape=jax.ShapeDtypeStruct((n, E_WIDTH), BF16),
        scratch_shapes=[
            pltpu.VMEM((w, 2 * w), F32),
            pltpu.VMEM((KH_STEP, 2, G_B // 2 * w, 2 * w), F32),
            pltpu.VMEM((KH_STEP, 2, G_B // 2 * w, 2 * w), BF16),
        ],
        compiler_params=_params("arbitrary", "arbitrary", "arbitrary"),
        name="swa_prompt",
    )(sink, zb, zb, kv_hm, kv_hm, kv_hm, kv_hm)


def _swa_sample_kernel(sink_ref, q_ref, g_ref, kn_ref, vn_ref, ck_ref, cv_ref, yprev_ref, y_ref, *, t_new, nbs):
    del yprev_ref
    wb = ck_ref.shape[1]
    nk = wb + t_new
    nr = G_B * t_new
    row = lax.broadcasted_iota(jnp.int32, (nr, nk), 0)
    col = lax.broadcasted_iota(jnp.int32, (nr, nk), 1)
    tok = row % t_new
    grp = row // t_new
    dist = wb + tok - col
    valid = (dist >= 0) & (dist < WINDOW)
    distf = dist.astype(F32)
    grp1 = lax.broadcasted_iota(jnp.int32, (nr, 1), 0) // t_new
    q_all = q_ref[...].astype(F32)
    g_all = g_ref[...].astype(F32)
    for kh in range(KV_B):
        ks = slice(kh * HD_B, (kh + 1) * HD_B)
        slope = jnp.exp(-LN2 * 8.0 * (kh * G_B + grp + 1).astype(F32) / H_B)
        sink = jnp.zeros((nr, 1), F32)
        for g in range(G_B):
            sink = jnp.where(grp1 == g, sink_ref[kh * G_B + g], sink)
        outs = []
        for bi in range(nbs):
            rb = slice(bi * t_new, (bi + 1) * t_new)
            kf = jnp.concatenate([ck_ref[bi, :, ks], kn_ref[kh][rb, :HD_B]], axis=0).astype(BF16)
            vf = jnp.concatenate([cv_ref[bi, :, ks], vn_ref[kh][rb, :HD_B]], axis=0).astype(BF16)
            qs = jnp.concatenate(
                [q_all[rb, (kh * G_B + g) * HD_B:(kh * G_B + g + 1) * HD_B] for g in range(G_B)], axis=0)
            s = lax.dot_general(qs.astype(BF16), kf, (((1,), (1,)), ((), ())), preferred_element_type=F32)
            s = s - slope * distf
            s = jnp.where(valid, s, NEG_BIG)
            m = jnp.maximum(jnp.max(s, axis=-1, keepdims=True), sink)
            p = jnp.where(valid, jnp.exp(s - m), 0.0)
            den = jnp.sum(p, axis=-1, keepdims=True) + jnp.exp(sink - m)
            outs.append(jnp.dot(p.astype(BF16), vf, preferred_element_type=F32) / den)
        for g in range(G_B):
            hsl = slice((kh * G_B + g) * HD_B, (kh * G_B + g + 1) * HD_B)
            o = jnp.concatenate([outs[bi][g * t_new:(g + 1) * t_new, :] for bi in range(nbs)], axis=0)
            y_ref[:, hsl] = (o * g_all[:, hsl]).astype(y_ref.dtype)


def swa_sample(zb, kv_hm, sink, y_prev, cache_k, cache_v, *, row0, n_batch, t_new):
    n = zb.shape[0]
    wb = cache_k.shape[1]
    nbs = SAMPLE_BATCHES
    tb = nbs * t_new
    rb0 = row0 // tb
    kern = functools.partial(_swa_sample_kernel, t_new=t_new, nbs=nbs)
    return pl.pallas_call(
        kern,
        grid=(n_batch // nbs,),
        in_specs=[
            pl.BlockSpec(memory_space=pltpu.SMEM),
            pl.BlockSpec((tb, E_WIDTH), lambda b: (rb0 + b, 0)),
            pl.BlockSpec((tb, E_WIDTH), lambda b: (rb0 + b, 1)),
            pl.BlockSpec((KV_B, tb, 2 * HD_B), lambda b: (0, rb0 + b, 0)),
            pl.BlockSpec((KV_B, tb, 2 * HD_B), lambda b: (1, rb0 + b, 0)),
            pl.BlockSpec((nbs, wb, KV_B * HD_B), lambda b: (b, 0, 0)),
            pl.BlockSpec((nbs, wb, KV_B * HD_B), lambda b: (b, 0, 0)),
            pl.BlockSpec(memory_space=pl.ANY),
        ],
        out_specs=pl.BlockSpec((tb, E_WIDTH), lambda b: (rb0 + b, 0)),
        out_shape=jax.ShapeDtypeStruct((n, E_WIDTH), BF16),
        input_output_aliases={7: 0},
        compiler_params=_params("arbitrary"),
        name="swa_sample",
    )(sink, zb, zb, kv_hm, kv_hm, cache_k, cache_v, y_prev)


def kernel(x_prompt, x_sample, state_hgrn, cache_k, cache_v, meta_tokens, norm_a, w_in_a, lb_a, onorm_a,
           w_out_a, norm_kv, w_kv, norm_b, w_in_b, sink_b, w_out_b, norm_f):
    bsz, seq, d = x_prompt.shape
    dec_b, dec_t, _ = x_sample.shape
    wb = cache_k.shape[1]

    rows_p = PAD + N_META + seq
    nb_p = rows_p // ROW_BLOCK
    n_p = bsz * rows_p
    n_s = dec_b * dec_t
    fd = 2 * H_A * DK_A
    s0_s = state_hgrn.astype(F32)
    ck = cache_k.reshape(dec_b, wb, KV_B * HD_B)
    cv = cache_v.reshape(dec_b, wb, KV_B * HD_B)

    x, xn = assemble(x_prompt, meta_tokens, x_sample.reshape(n_s, d), norm_a, 0)
    sp = ss = None
    for layer in range(N_A):
        zqf = proj(xn, w_in_a, layer, 0, ((fd // 2, "silu_dk"), (fd // 2, "logf")), F32, lb_a=lb_a)
        zig, w_out16 = proj(xn, w_in_a, layer, fd, ((E_WIDTH, "id"), (E_WIDTH, "silu")), BF16, cast_all=w_out_a)
        y, sp = hgrn_scan(zqf, zig, onorm_a, None, None, sp, layer=layer, n_batch=bsz, rows_per_batch=rows_p,
                          row0=0, chunk=64, n_chunks=ROW_BLOCK // 64, hb=16, per_batch=False)
        y, ss = hgrn_scan(zqf, zig, onorm_a, s0_s, y, ss, layer=layer, n_batch=dec_b, rows_per_batch=dec_t,
                          row0=n_p, chunk=dec_t, n_chunks=SAMPLE_BATCHES, hb=8, per_batch=True)
        if layer + 1 < N_A:
            x, xn = matmul_residual(y, w_out16, x, (norm_a[layer + 1],), BF16)
        else:
            x, xn, xn_kv = matmul_residual(y, w_out16, x, (norm_b[0], norm_kv), BF16)
    kv_hm = kv_proj(xn_kv, w_kv)
    for layer in range(N_B):
        zb, w_out16 = proj(xn, w_in_b, layer, 0, ((E_WIDTH, "scale_hd"), (E_WIDTH, "silu")), BF16,
                           cast_all=w_out_b)
        y = swa_prompt(zb, kv_hm, sink_b[layer], bsz, nb_p)
        y = swa_sample(zb, kv_hm, sink_b[layer], y, ck, cv, row0=n_p, n_batch=dec_b, t_new=dec_t)
        if layer + 1 < N_B:
            x, xn = matmul_residual(y, w_out16, x, (norm_b[layer + 1],), BF16)
        else:
            out_p, out_s = matmul_residual(y, w_out16, x, (norm_f,), F32, keep_x=False, split_rows=n_p)

    y_prompt = out_p.reshape(bsz, rows_p, d)[:, PAD + N_META:]
    y_sample = out_s.reshape(dec_b, dec_t, d)
    kvp = jnp.stack([kv_hm[:, (b + 1) * rows_p - wb:(b + 1) * rows_p, :HD_B] for b in range(bsz)], axis=1)
    kvp = kvp.reshape(2, KV_B, bsz, wb, HD_B)
    cache_k_prompt = jnp.transpose(kvp[0], (1, 2, 0, 3)).astype(cache_k.dtype)
    cache_v_prompt = jnp.transpose(kvp[1], (1, 2, 0, 3)).astype(cache_v.dtype)
    kvs = kv_hm[:, n_p:, :HD_B].reshape(2, KV_B, dec_b, dec_t, HD_B)
    k_new = jnp.transpose(kvs[0], (1, 2, 0, 3)).astype(cache_k.dtype)
    v_new = jnp.transpose(kvs[1], (1, 2, 0, 3)).astype(cache_v.dtype)
    cache_k_sample = jnp.concatenate([cache_k, k_new], axis=1)[:, -wb:]
    cache_v_sample = jnp.concatenate([cache_v, v_new], axis=1)[:, -wb:]

    return (y_prompt, y_sample, sp.astype(state_hgrn.dtype), cache_k_prompt, cache_v_prompt,
            ss.astype(state_hgrn.dtype), cache_k_sample, cache_v_sample)
```

```python
import functools
import math

import jax
import jax.numpy as jnp
from jax import lax
from jax.experimental import pallas as pl
from jax.experimental.pallas import tpu as pltpu

D_MODEL = 2048
E_WIDTH = 2 * D_MODEL
N_A = 2
N_B = 2
DK_A = 128
H_A = D_MODEL // DK_A
DV_A = E_WIDTH // H_A
HD_B = 64
H_B = E_WIDTH // HD_B
KV_B = H_B // 8
G_B = H_B // KV_B
WINDOW = 128
N_META = 16
EPS = 1e-6
NEG_BIG = -1e30
TINY = 1e-30

LANES = 128
BF16_SUBLANES = 16
ROW_BLOCK = 128
PAD = ROW_BLOCK - N_META
FACTOR_SAFE = 60.0
LN2 = 0.6931471805599453
VMEM_LIMIT = 56 * 1024 * 1024
KH_STEP = 8
SAMPLE_BATCHES = BF16_SUBLANES // 8
PROJ_ROWS = 1120
RES_ROWS = 448
PROJ_COLS = 1024
PROJ_SUBTILES = 5

F32 = jnp.float32
BF16 = jnp.bfloat16


def _sigmoid(x):
    return 1.0 / (1.0 + jnp.exp(-x))


def _silu(x):
    return x * _sigmoid(x)


def _params(*sem):
    return pltpu.CompilerParams(dimension_semantics=sem, vmem_limit_bytes=VMEM_LIMIT)


def _row_tile(n, cap):
    for t in range(min(cap, n) // BF16_SUBLANES * BF16_SUBLANES, 0, -BF16_SUBLANES):
        if n % t == 0:
            return t
    raise ValueError(f"no row tile for {n}")


def _prompt_block(b, i, n_batch, blocks_per_batch):
    return jnp.where(i == 0, n_batch * (blocks_per_batch - 1) + b, b * (blocks_per_batch - 1) + i - 1)


def _assemble_kernel(xp_ref, meta_ref, xs_ref, g_ref, x_ref, xn_ref, *, n_prompt_blocks, n_batch):
    r = pl.program_id(0)
    is_prompt = r < n_prompt_blocks
    is_meta = jnp.logical_not(is_prompt) & (r < n_prompt_blocks + n_batch)

    def emit(x):
        x_ref[...] = x
        ms = jnp.mean(x * x, axis=-1, keepdims=True)
        xn_ref[...] = (x * lax.rsqrt(ms + EPS) * g_ref[...]).astype(xn_ref.dtype)

    @pl.when(is_meta)
    def _():
        emit(jnp.concatenate([jnp.zeros((PAD, x_ref.shape[1]), F32), meta_ref[...].astype(F32)], axis=0))

    @pl.when(is_prompt)
    def _():
        emit(xp_ref[...].astype(F32))

    @pl.when(jnp.logical_not(is_prompt | is_meta))
    def _():
        emit(xs_ref[...].astype(F32))


def assemble(x_prompt, meta_tokens, x_sample2d, g_all, layer):
    bsz, seq, d = x_prompt.shape
    n_s = x_sample2d.shape[0]
    assert seq % ROW_BLOCK == 0 and n_s % ROW_BLOCK == 0, (seq, n_s)
    nbp = seq // ROW_BLOCK
    npb = bsz * nbp
    nsb = n_s // ROW_BLOCK
    n = (npb + bsz + nsb) * ROW_BLOCK
    kern = functools.partial(_assemble_kernel, n_prompt_blocks=npb, n_batch=bsz)
    row_spec = pl.BlockSpec((ROW_BLOCK, d), lambda r: (r, 0))
    return pl.pallas_call(
        kern,
        grid=(npb + bsz + nsb,),
        in_specs=[
            pl.BlockSpec((None, ROW_BLOCK, d), lambda r: (jnp.minimum(r // nbp, bsz - 1), r % nbp, 0)),
            pl.BlockSpec((N_META, d), lambda r: (0, 0)),
            pl.BlockSpec((ROW_BLOCK, d), lambda r: (jnp.clip(r - npb - bsz, 0, nsb - 1), 0)),
            pl.BlockSpec((None, 1, d), lambda r: (layer, 0, 0)),
        ],
        out_specs=[row_spec, row_spec],
        out_shape=[jax.ShapeDtypeStruct((n, d), F32), jax.ShapeDtypeStruct((n, d), BF16)],
        compiler_params=_params("arbitrary"),
        name="assemble",
    )(x_prompt, meta_tokens, x_sample2d, g_all.reshape(g_all.shape[0], 1, d))


def _lower_bound(lbp, layer):
    e = jnp.exp(lbp - jnp.max(lbp, axis=0, keepdims=True))
    p = e / jnp.sum(e, axis=0, keepdims=True)
    acc = p[0:1]
    for j in range(1, layer + 1):
        acc = acc + p[j:j + 1]
    return acc - p[0:1]


def _log_forget(f, lb, case):
    ef = jnp.exp(-jnp.abs(f))
    r = 1.0 / (1.0 + ef)
    if case == "none":
        return jnp.log(r) + (jnp.log1p(-lb) + jnp.minimum(f, 0.0))
    sig = jnp.where(f >= 0, 1.0, ef) * r
    arg_pos = jnp.maximum(lb, TINY) + (1.0 - lb) * sig
    if case == "all":
        return jnp.log(arg_pos)
    pos = lb > 0
    arg = jnp.where(pos, arg_pos, r)
    off = jnp.where(pos, 0.0, jnp.log1p(-lb) + jnp.minimum(f, 0.0))
    return jnp.log(arg) + off


def _proj_kernel(*refs, segments, layer, with_cast):
    refs = list(refs)
    x_ref, w_ref = refs[:2]
    pos = 2
    lb_ref = cast_src_ref = cast_dst_ref = None
    if any(mode == "logf" for _, _, mode in segments):
        lb_ref = refs[pos]
        pos += 1
    if with_cast:
        cast_src_ref = refs[pos]
        pos += 1
    o_ref = refs[pos]
    pos += 1
    if with_cast:
        cast_dst_ref = refs[pos]
        pos += 1
    wb_ref = refs[pos]
    j = pl.program_id(0)

    if with_cast:
        cast_dst_ref[...] = cast_src_ref[...].astype(BF16)

    @pl.when(pl.program_id(1) == 0)
    def _():
        wb_ref[...] = w_ref[...].astype(BF16)

    def tile(act):
        tm = x_ref.shape[0]
        nsub = PROJ_SUBTILES * max(tm // PROJ_ROWS, 1)
        sub = tm // nsub if tm % (nsub * BF16_SUBLANES) == 0 else tm
        for r0 in range(0, tm, sub):
            acc = jnp.dot(x_ref[r0:r0 + sub, :], wb_ref[...], preferred_element_type=F32)
            o_ref[r0:r0 + sub, :] = act(acc).astype(o_ref.dtype)

    acts = {
        "id": lambda a: a,
        "silu": _silu,
        "silu_dk": lambda a: _silu(a) * (DK_A ** -0.5),
        "scale_hd": lambda a: a * (HD_B ** -0.5),
    }
    for j0, j1, mode in segments:
        @pl.when((j >= j0) & (j < j1))
        def _(mode=mode):
            if mode != "logf":
                tile(acts[mode])
                return
            lb = _lower_bound(lb_ref[...], layer)
            n_pos = jnp.sum((lb > 0).astype(jnp.int32))
            cases = (("all", n_pos == lb.shape[1]), ("none", n_pos == 0),
                     ("mixed", (n_pos > 0) & (n_pos < lb.shape[1])))
            for case, cond in cases:
                @pl.when(cond)
                def _(case=case):
                    tile(lambda a: _log_forget(a, lb, case))


def proj(xn, w_all, layer, col0, segments, out_dtype, lb_a=None, cast_all=None):
    n, d = xn.shape
    tn = PROJ_COLS
    tm = _row_tile(n, PROJ_ROWS * (4 // jnp.dtype(out_dtype).itemsize))
    jb = col0 // tn
    bounds, j0 = [], 0
    for ncols_seg, mode in segments:
        bounds.append((j0, j0 + ncols_seg // tn, mode))
        j0 += ncols_seg // tn
    ncols = j0 * tn
    in_specs = [
        pl.BlockSpec((tm, d), lambda j, i: (i, 0)),
        pl.BlockSpec((None, d, tn), lambda j, i: (layer, 0, jb + j)),
    ]
    args = [xn, w_all]
    for ja, jz, mode in bounds:
        if mode == "logf":
            in_specs.append(pl.BlockSpec((lb_a.shape[0], tn),
                                         lambda j, i, ja=ja, jz=jz: (0, jnp.clip(j - ja, 0, jz - ja - 1))))
            args.append(lb_a)
    nj, ni = ncols // tn, n // tm
    out_specs = [pl.BlockSpec((tm, tn), lambda j, i: (i, j))]
    out_shape = [jax.ShapeDtypeStruct((n, ncols), out_dtype)]
    if cast_all is not None:
        _, ce, cd = cast_all.shape
        slab = ce // (nj * ni)
        assert slab * nj * ni == ce and slab % BF16_SUBLANES == 0, (ce, nj, ni)
        in_specs.append(pl.BlockSpec((None, slab, cd), lambda j, i: (layer, j * ni + i, 0)))
        args.append(cast_all)
        out_specs.append(pl.BlockSpec((slab, cd), lambda j, i: (j * ni + i, 0)))
        out_shape.append(jax.ShapeDtypeStruct((ce, cd), BF16))
    kern = functools.partial(_proj_kernel, segments=tuple(bounds), layer=layer, with_cast=cast_all is not None)
    res = pl.pallas_call(
        kern,
        grid=(nj, ni),
        in_specs=in_specs,
        out_specs=out_specs,
        out_shape=out_shape,
        scratch_shapes=[pltpu.VMEM((d, tn), BF16)],
        compiler_params=_params("arbitrary", "arbitrary"),
        name="proj",
    )(*args)
    return res if cast_all is not None else res[0]


def _kv_proj_kernel(x_ref, w_ref, o_ref, wb_ref):
    @pl.when(pl.program_id(0) == 0)
    def _():
        wb_ref[...] = w_ref[...].astype(BF16)

    kv = jnp.dot(x_ref[...], wb_ref[...], preferred_element_type=F32)
    ones = jnp.ones((kv.shape[0], HD_B), F32)
    for j in range(KV_B):
        k_j = kv[:, j * HD_B:(j + 1) * HD_B]
        o_ref[j] = jnp.concatenate([k_j, k_j], axis=1)
        v_j = kv[:, (KV_B + j) * HD_B:(KV_B + j + 1) * HD_B]
        o_ref[KV_B + j] = jnp.concatenate([v_j, ones], axis=1)


def kv_proj(xn, w):
    n, d = xn.shape
    tm = _row_tile(n, RES_ROWS)
    return pl.pallas_call(
        _kv_proj_kernel,
        grid=(n // tm,),
        in_specs=[
            pl.BlockSpec((tm, d), lambda i: (i, 0)),
            pl.BlockSpec((d, 2 * KV_B * HD_B), lambda i: (0, 0), pipeline_mode=pl.Buffered(1)),
        ],
        out_specs=pl.BlockSpec((2 * KV_B, tm, 2 * HD_B), lambda i: (0, i, 0)),
        out_shape=jax.ShapeDtypeStruct((2 * KV_B, n, 2 * HD_B), F32),
        scratch_shapes=[pltpu.VMEM((d, 2 * KV_B * HD_B), BF16)],
        compiler_params=_params("arbitrary"),
        name="kv_proj",
    )(xn, w)


def _matmul_res_kernel(y_ref, w_ref, x_ref, g_ref, *out_refs, n_norm, keep_x, head_tiles):
    x = x_ref[...] + jnp.dot(y_ref[...], w_ref[...], preferred_element_type=F32)
    ms = jnp.mean(x * x, axis=-1, keepdims=True)
    xh = x * lax.rsqrt(ms + EPS)
    if head_tiles is not None:
        head_ref, tail_ref = out_refs
        i = pl.program_id(0)

        @pl.when(i < head_tiles)
        def _():
            head_ref[...] = xh * g_ref[0]

        @pl.when(i >= head_tiles)
        def _():
            tail_ref[...] = xh * g_ref[0]
        return
    pos = 0
    if keep_x:
        out_refs[0][...] = x
        pos = 1
    for t in range(n_norm):
        out_refs[pos + t][...] = (xh * g_ref[t]).astype(out_refs[pos + t].dtype)


def matmul_residual(y, w, x, gains, norm_dtype, keep_x=True, split_rows=None):
    n, e = y.shape
    d = w.shape[1]
    gs = jnp.stack([g.reshape(1, d) for g in gains])
    if split_rows is None:
        tm = _row_tile(n, RES_ROWS)
        head_tiles = None
        row_spec = pl.BlockSpec((tm, d), lambda i: (i, 0))
        out_shapes = ([jax.ShapeDtypeStruct((n, d), F32)] if keep_x else []) + \
            [jax.ShapeDtypeStruct((n, d), norm_dtype) for _ in gains]
        out_specs = [row_spec] * len(out_shapes)
    else:
        assert len(gains) == 1 and not keep_x and norm_dtype == F32
        tm = _row_tile(math.gcd(split_rows, n - split_rows), RES_ROWS)
        head_tiles = split_rows // tm
        row_spec = pl.BlockSpec((tm, d), lambda i: (i, 0))
        out_shapes = [jax.ShapeDtypeStruct((split_rows, d), F32), jax.ShapeDtypeStruct((n - split_rows, d), F32)]
        out_specs = [pl.BlockSpec((tm, d), lambda i: (jnp.minimum(i, head_tiles - 1), 0)),
                     pl.BlockSpec((tm, d), lambda i: (jnp.maximum(i - head_tiles, 0), 0))]
    kern = functools.partial(_matmul_res_kernel, n_norm=len(gains), keep_x=keep_x, head_tiles=head_tiles)
    return pl.pallas_call(
        kern,
        grid=(n // tm,),
        in_specs=[
            pl.BlockSpec((tm, e), lambda i: (i, 0)),
            pl.BlockSpec((e, d), lambda i: (0, 0), pipeline_mode=pl.Buffered(1)),
            row_spec,
            pl.BlockSpec((len(gains), 1, d), lambda i: (0, 0, 0)),
        ],
        out_specs=out_specs,
        out_shape=out_shapes,
        compiler_params=_params("arbitrary"),
        name="matmul_residual",
    )(y, w, x, gs)


def _hgrn_kernel(*refs, chunk, n_chunks, hb, per_batch, zero_init, n_alias):
    refs = list(refs)
    zq_ref, zf_ref, zi_ref, zg_ref, on_ref = refs[:5]
    pos = 5
    s0_ref = None
    if not zero_init:
        s0_ref = refs[pos]
        pos += 1
    pos += n_alias
    y_ref, sfin_ref, s_ref, cum_sc, oint_sc, ystage_sc, flag_ref = refs[pos:]
    c = chunk
    t = pl.program_id(2)
    aligned = c % BF16_SUBLANES == 0

    @pl.when(t == 0)
    def _():
        if zero_init:
            s_ref[...] = jnp.zeros_like(s_ref)
        else:
            s_ref[...] = s0_ref[...]

    on_g = on_ref[...]
    if aligned:
        zi_all = zg_all = None
    else:
        zi_all = zi_ref[...].astype(F32)
        zg_all = zg_ref[...].astype(F32)

    ri = lax.broadcasted_iota(jnp.int32, (c, c), 0)
    ci_ = lax.broadcasted_iota(jnp.int32, (c, c), 1)
    tril = (ci_ <= ri).astype(BF16)
    row_c = lax.broadcasted_iota(jnp.int32, (c, LANES), 0)
    col_c = lax.broadcasted_iota(jnp.int32, (c, LANES), 1)
    causal = col_c <= row_c
    zpad_k = jnp.zeros((LANES - c, DK_A), F32)
    zpad_x = jnp.zeros((LANES - c - 8, DK_A), F32)

    def values(rows, vs_):
        return zi_ref[rows, vs_].astype(F32) if aligned else zi_all[rows, vs_]

    def values_padded(rows, vs_):
        return jnp.concatenate([zi_all[rows, vs_], jnp.zeros((LANES - c, DV_A), F32)], axis=0).astype(BF16)

    def emit_y(o, rows, vs_):
        ms = jnp.mean(o * o, axis=-1, keepdims=True)
        o = o * lax.rsqrt(ms + EPS) * on_g
        if aligned:
            y_ref[rows, vs_] = (o * zg_ref[rows, vs_].astype(F32)).astype(y_ref.dtype)
        else:
            ystage_sc[rows, vs_] = o * zg_all[rows, vs_]

    for ci in range(n_chunks):
        rows = slice(ci * c, (ci + 1) * c)
        si = ci if per_batch else 0
        logf = zf_ref[rows, :]
        h1 = logf.astype(BF16)
        h2 = (logf - h1.astype(F32)).astype(BF16)
        cum = jnp.dot(tril, h1, preferred_element_type=F32) + jnp.dot(tril, h2, preferred_element_type=F32)
        last = cum[c - 1:c, :]
        mid = cum[c // 2 - 1:c // 2, :]
        dev = jnp.max(jnp.maximum(cum[0:1, :] - mid, mid - last))
        flag_ref[ci] = (dev <= FACTOR_SAFE).astype(jnp.int32)
        cum_sc[rows, :] = cum

        for h in range(hb):
            hs = slice(h * DK_A, (h + 1) * DK_A)
            vs_ = slice(h * DV_A, (h + 1) * DV_A)
            q_h = zq_ref[rows, hs]
            k_h = 1.0 - jnp.exp(zf_ref[rows, hs])
            cum_h = cum_sc[rows, hs]
            d_h = cum_h - mid[:, hs]
            qm = (q_h * jnp.exp(d_h)).astype(BF16)
            km = k_h * jnp.exp(-d_h)
            s_prev = s_ref[si, h]
            qa = (q_h * jnp.exp(cum_h)).astype(BF16)
            o_inter = jnp.dot(qa, s_prev.astype(BF16), preferred_element_type=F32)
            oint_sc[rows, vs_] = o_inter
            kd = k_h * jnp.exp(last[:, hs] - cum_h)
            el8 = jnp.broadcast_to(jnp.exp(last[:, hs]), (8, DK_A))
            x_t = jnp.concatenate([kd, el8, zpad_x], axis=0).T
            if aligned:
                att = lax.dot_general(qm, km.astype(BF16), (((1,), (1,)), ((), ())), preferred_element_type=F32)
                att = jnp.where(causal[:, :c], att, 0.0).astype(BF16)
                lhs = jnp.concatenate([att, x_t[:, :c].astype(BF16)], axis=0)
                res = jnp.dot(lhs, zi_ref[rows, vs_], preferred_element_type=F32)
                o_intra, ds = res[:c], res[c:]
            else:
                v_pad = values_padded(rows, vs_)
                km_pad = jnp.concatenate([km, zpad_k], axis=0).astype(BF16)
                att = lax.dot_general(qm, km_pad, (((1,), (1,)), ((), ())), preferred_element_type=F32)
                att = jnp.where(causal, att, 0.0).astype(BF16)
                o_intra = jnp.dot(att, v_pad, preferred_element_type=F32)
                ds = jnp.dot(x_t.astype(BF16), v_pad, preferred_element_type=F32)
            s_ref[si, h] = x_t[:, c:c + 1] * s_prev + ds
            emit_y(o_inter + o_intra, rows, vs_)

    for ci in range(n_chunks):
        rows = slice(ci * c, (ci + 1) * c)

        @pl.when(flag_ref[ci] == 0)
        def _(rows=rows):
            row_v = lax.broadcasted_iota(jnp.int32, (c, 1), 0)

            def row_of(a, s):
                return jnp.sum(jnp.where(row_v == s, a, 0.0), axis=0, keepdims=True)

            for h in range(hb):
                hs = slice(h * DK_A, (h + 1) * DK_A)
                vs_ = slice(h * DV_A, (h + 1) * DV_A)
                q_h = zq_ref[rows, hs]
                k_h = 1.0 - jnp.exp(zf_ref[rows, hs])
                cum_h = cum_sc[rows, hs]
                v_h = values(rows, vs_)

                def body(s, acc_o, q_h=q_h, k_h=k_h, cum_h=cum_h, v_h=v_h):
                    dec = jnp.exp(jnp.minimum(cum_h - row_of(cum_h, s), 0.0))
                    w = jnp.sum(q_h * row_of(k_h, s) * dec, axis=-1, keepdims=True)
                    w = jnp.where(row_v >= s, w, 0.0)
                    return acc_o + w * row_of(v_h, s)

                o_intra = lax.fori_loop(0, c, body, jnp.zeros((c, DV_A), F32))
                emit_y(oint_sc[rows, vs_] + o_intra, rows, vs_)

    if not aligned:
        y_ref[...] = ystage_sc[...].astype(y_ref.dtype)

    @pl.when(t == pl.num_programs(2) - 1)
    def _():
        sfin_ref[...] = s_ref[...]


def hgrn_scan(zqf, zig, onorm_a, s0_all, y_prev, states_prev, *, layer, n_batch, rows_per_batch, row0, chunk,
              n_chunks, hb, per_batch):
    n = zqf.shape[0]
    tb = chunk * n_chunks
    ns = n_chunks if per_batch else 1
    nt = 1 if per_batch else rows_per_batch // tb
    nbg = n_batch // ns
    ng = H_A // hb
    wk, wv = hb * DK_A, hb * DV_A
    f_off = (H_A * DK_A) // wk
    g_off = E_WIDTH // wv
    if per_batch:
        rb0 = row0 // tb
        row = lambda b, g, t: rb0 + b
    else:
        assert tb == ROW_BLOCK and row0 == 0
        row = lambda b, g, t: _prompt_block(b, t, n_batch, nt)
    in_specs = [
        pl.BlockSpec((tb, wk), lambda b, g, t: (row(b, g, t), g)),
        pl.BlockSpec((tb, wk), lambda b, g, t: (row(b, g, t), f_off + g)),
        pl.BlockSpec((tb, wv), lambda b, g, t: (row(b, g, t), g)),
        pl.BlockSpec((tb, wv), lambda b, g, t: (row(b, g, t), g_off + g)),
        pl.BlockSpec((None, 1, DV_A), lambda b, g, t: (layer, 0, 0)),
    ]
    args = [zqf, zqf, zig, zig, onorm_a.reshape(N_A, 1, DV_A)]
    state_spec = pl.BlockSpec((None, ns, hb, DK_A, DV_A), lambda b, g, t: (layer, b, g, 0, 0))
    if s0_all is not None:
        in_specs.append(state_spec)
        args.append(s0_all)
    aliases = {}
    for out_idx, prev in ((0, y_prev), (1, states_prev)):
        if prev is not None:
            aliases[len(args)] = out_idx
            in_specs.append(pl.BlockSpec(memory_space=pl.ANY))
            args.append(prev)
    kern = functools.partial(_hgrn_kernel, chunk=chunk, n_chunks=n_chunks, hb=hb, per_batch=per_batch,
                             zero_init=s0_all is None, n_alias=len(aliases))
    return pl.pallas_call(
        kern,
        grid=(nbg, ng, nt),
        in_specs=in_specs,
        out_specs=[
            pl.BlockSpec((tb, wv), lambda b, g, t: (row(b, g, t), g)),
            state_spec,
        ],
        out_shape=[
            jax.ShapeDtypeStruct((n, E_WIDTH), BF16),
            jax.ShapeDtypeStruct((N_A, n_batch, H_A, DK_A, DV_A), F32),
        ],
        scratch_shapes=[
            pltpu.VMEM((ns, hb, DK_A, DV_A), F32),
            pltpu.VMEM((tb, wk), F32),
            pltpu.VMEM((tb, wv), F32),
            pltpu.VMEM((tb, wv), F32),
            pltpu.SMEM((n_chunks,), jnp.int32),
        ],
        input_output_aliases=aliases,
        compiler_params=_params("arbitrary", "arbitrary", "arbitrary"),
        name=f"hgrn_scan_c{chunk}",
    )(*args)


def _swa_prompt_kernel(sink_ref, q_ref, g_ref, kc_ref, kp_ref, vc_ref, vp_ref, y_ref, d_sc, s_sc, p_sc):
    blk = pl.program_id(1)
    kg = pl.program_id(2)
    w = WINDOW
    npair = G_B // 2

    @pl.when(kg == 0)
    def _():
        row = lax.broadcasted_iota(jnp.int32, (w, 2 * w), 0)
        col = lax.broadcasted_iota(jnp.int32, (w, 2 * w), 1)
        dist = w + row - col
        key_pos = (blk - 1) * w + col
        valid = (dist >= 0) & (dist < w) & (key_pos >= PAD)
        d_sc[...] = jnp.where(valid, dist.astype(F32), -NEG_BIG)

    lane = lax.broadcasted_iota(jnp.int32, (w, LANES), 1)
    lane_k = lax.broadcasted_iota(jnp.int32, (2 * w, LANES), 1)
    row_k = lax.broadcasted_iota(jnp.int32, (2 * w, LANES), 0)
    for kk in range(KH_STEP):
        kh = kg * KH_STEP + kk
        kdup = jnp.concatenate([kp_ref[kk], kc_ref[kk]], axis=0)
        k_par = [jnp.where(lane_k < HD_B, kdup, 0.0).astype(BF16), jnp.where(lane_k < HD_B, 0.0, kdup).astype(BF16)]
        vraw = jnp.concatenate([vp_ref[kk], vc_ref[kk]], axis=0)
        vext = jnp.where(row_k == 0, jnp.where(lane_k < HD_B, 0.0, vraw), vraw).astype(BF16)
        qs = jnp.concatenate(
            [q_ref[:, (kk * npair + j) * LANES:(kk * npair + j + 1) * LANES] for j in range(npair)],
            axis=0).astype(BF16)
        for par in range(2):
            s_sc[kk, par] = lax.dot_general(qs, k_par[par], (((1,), (1,)), ((), ())),
                                            preferred_element_type=F32)
        for j in range(npair):
            rows = slice(j * w, (j + 1) * w)
            for par in range(2):
                head = kh * G_B + 2 * j + par
                slope = jnp.exp(-LN2 * 8.0 * (head + 1).astype(F32) / H_B)
                sink = sink_ref[head]
                s_lo = s_sc[kk, par, rows, :w] - slope * d_sc[:, :w]
                s_lo = jnp.where(lane == 0, sink, s_lo)
                s_hi = s_sc[kk, par, rows, w:] - slope * d_sc[:, w:]
                m = jnp.max(jnp.maximum(s_lo, s_hi), axis=-1, keepdims=True)
                p_sc[kk, par, rows, :w] = jnp.exp(s_lo - m).astype(BF16)
                p_sc[kk, par, rows, w:] = jnp.exp(s_hi - m).astype(BF16)
        o_par = [jnp.dot(p_sc[kk, par], vext, preferred_element_type=F32) for par in range(2)]
        for j in range(npair):
            o_even = o_par[0][j * w:(j + 1) * w, :]
            o_odd = o_par[1][j * w:(j + 1) * w, :]
            num = jnp.where(lane < HD_B, o_even, pltpu.roll(o_odd, HD_B, 1))
            den = jnp.where(lane < HD_B, pltpu.roll(o_even, HD_B, 1), o_odd)
            cs = slice((kk * npair + j) * LANES, (kk * npair + j + 1) * LANES)
            y_ref[:, cs] = (num / den * g_ref[:, cs].astype(F32)).astype(BF16)


def swa_prompt(zb, kv_hm, sink, n_batch, blocks_per_batch):
    n = zb.shape[0]
    w = WINDOW
    gw = KH_STEP * G_B * HD_B
    nb = blocks_per_batch
    nkg = KV_B // KH_STEP
    cur = lambda b, i, kg: _prompt_block(b, i, n_batch, nb)
    prev = lambda b, i, kg: _prompt_block(b, jnp.maximum(i - 1, 0), n_batch, nb)
    kv_block = (KH_STEP, w, 2 * HD_B)
    return pl.pallas_call(
        _swa_prompt_kernel,
        grid=(n_batch, nb, nkg),
        in_specs=[
            pl.BlockSpec(memory_space=pltpu.SMEM),
            pl.BlockSpec((w, gw), lambda b, i, kg: (cur(b, i, kg), kg)),
            pl.BlockSpec((w, gw), lambda b, i, kg: (cur(b, i, kg), nkg + kg)),
            pl.BlockSpec(kv_block, lambda b, i, kg: (kg, cur(b, i, kg), 0)),
            pl.BlockSpec(kv_block, lambda b, i, kg: (kg, prev(b, i, kg), 0)),
            pl.BlockSpec(kv_block, lambda b, i, kg: (nkg + kg, cur(b, i, kg), 0)),
            pl.BlockSpec(kv_block, lambda b, i, kg: (nkg + kg, prev(b, i, kg), 0)),
        ],
        out_specs=pl.BlockSpec((w, gw), lambda b, i, kg: (cur(b, i, kg), kg)),
        out_shape=jax.ShapeDtypeStruct((n, E_WIDTH), BF16),
        scratch_shapes=[
            pltpu.VMEM((w, 2 * w), F32),
            pltpu.VMEM((KH_STEP, 2, G_B // 2 * w, 2 * w), F32),
            pltpu.VMEM((KH_STEP, 2, G_B // 2 * w, 2 * w), BF16),
        ],
        compiler_params=_params("arbitrary", "arbitrary", "arbitrary"),
        name="swa_prompt",
    )(sink, zb, zb, kv_hm, kv_hm, kv_hm, kv_hm)


def _swa_sample_kernel(sink_ref, q_ref, g_ref, kn_ref, vn_ref, ck_ref, cv_ref, yprev_ref, y_ref, *, t_new, nbs):
    del yprev_ref
    wb = ck_ref.shape[1]
    nk = wb + t_new
    nr = G_B * t_new
    row = lax.broadcasted_iota(jnp.int32, (nr, nk), 0)
    col = lax.broadcasted_iota(jnp.int32, (nr, nk), 1)
    tok = row % t_new
    grp = row // t_new
    dist = wb + tok - col
    valid = (dist >= 0) & (dist < WINDOW)
    distf = dist.astype(F32)
    grp1 = lax.broadcasted_iota(jnp.int32, (nr, 1), 0) // t_new
    q_all = q_ref[...].astype(F32)
    g_all = g_ref[...].astype(F32)
    for kh in range(KV_B):
        ks = slice(kh * HD_B, (kh + 1) * HD_B)
        slope = jnp.exp(-LN2 * 8.0 * (kh * G_B + grp + 1).astype(F32) / H_B)
        sink = jnp.zeros((nr, 1), F32)
        for g in range(G_B):
            sink = jnp.where(grp1 == g, sink_ref[kh * G_B + g], sink)
        outs = []
        for bi in range(nbs):
            rb = slice(bi * t_new, (bi + 1) * t_new)
            kf = jnp.concatenate([ck_ref[bi, :, ks], kn_ref[kh][rb, :HD_B]], axis=0).astype(BF16)
            vf = jnp.concatenate([cv_ref[bi, :, ks], vn_ref[kh][rb, :HD_B]], axis=0).astype(BF16)
            qs = jnp.concatenate(
                [q_all[rb, (kh * G_B + g) * HD_B:(kh * G_B + g + 1) * HD_B] for g in range(G_B)], axis=0)
            s = lax.dot_general(qs.astype(BF16), kf, (((1,), (1,)), ((), ())), preferred_element_type=F32)
            s = s - slope * distf
            s = jnp.where(valid, s, NEG_BIG)
            m = jnp.maximum(jnp.max(s, axis=-1, keepdims=True), sink)
            p = jnp.where(valid, jnp.exp(s - m), 0.0)
            den = jnp.sum(p, axis=-1, keepdims=True) + jnp.exp(sink - m)
            outs.append(jnp.dot(p.astype(BF16), vf, preferred_element_type=F32) / den)
        for g in range(G_B):
            hsl = slice((kh * G_B + g) * HD_B, (kh * G_B + g + 1) * HD_B)
            o = jnp.concatenate([outs[bi][g * t_new:(g + 1) * t_new, :] for bi in range(nbs)], axis=0)
            y_ref[:, hsl] = (o * g_all[:, hsl]).astype(y_ref.dtype)


def swa_sample(zb, kv_hm, sink, y_prev, cache_k, cache_v, *, row0, n_batch, t_new):
    n = zb.shape[0]
    wb = cache_k.shape[1]
    nbs = SAMPLE_BATCHES
    tb = nbs * t_new
    rb0 = row0 // tb
    kern = functools.partial(_swa_sample_kernel, t_new=t_new, nbs=nbs)
    return pl.pallas_call(
        kern,
        grid=(n_batch // nbs,),
        in_specs=[
            pl.BlockSpec(memory_space=pltpu.SMEM),
            pl.BlockSpec((tb, E_WIDTH), lambda b: (rb0 + b, 0)),
            pl.BlockSpec((tb, E_WIDTH), lambda b: (rb0 + b, 1)),
            pl.BlockSpec((KV_B, tb, 2 * HD_B), lambda b: (0, rb0 + b, 0)),
            pl.BlockSpec((KV_B, tb, 2 * HD_B), lambda b: (1, rb0 + b, 0)),
            pl.BlockSpec((nbs, wb, KV_B * HD_B), lambda b: (b, 0, 0)),
            pl.BlockSpec((nbs, wb, KV_B * HD_B), lambda b: (b, 0, 0)),
            pl.BlockSpec(memory_space=pl.ANY),
        ],
        out_specs=pl.BlockSpec((tb, E_WIDTH), lambda b: (rb0 + b, 0)),
        out_shape=jax.ShapeDtypeStruct((n, E_WIDTH), BF16),
        input_output_aliases={7: 0},
        compiler_params=_params("arbitrary"),
        name="swa_sample",
    )(sink, zb, zb, kv_hm, kv_hm, cache_k, cache_v, y_prev)


def kernel(x_prompt, x_sample, state_hgrn, cache_k, cache_v, meta_tokens, norm_a, w_in_a, lb_a, onorm_a,
           w_out_a, norm_kv, w_kv, norm_b, w_in_b, sink_b, w_out_b, norm_f):
    bsz, seq, d = x_prompt.shape
    dec_b, dec_t, _ = x_sample.shape
    wb = cache_k.shape[1]

    rows_p = PAD + N_META + seq
    nb_p = rows_p // ROW_BLOCK
    n_p = bsz * rows_p
    n_s = dec_b * dec_t
    fd = 2 * H_A * DK_A
    s0_s = state_hgrn.astype(F32)
    ck = cache_k.reshape(dec_b, wb, KV_B * HD_B)
    cv = cache_v.reshape(dec_b, wb, KV_B * HD_B)

    x, xn = assemble(x_prompt, meta_tokens, x_sample.reshape(n_s, d), norm_a, 0)
    sp = ss = None
    for layer in range(N_A):
        zqf = proj(xn, w_in_a, layer, 0, ((fd // 2, "silu_dk"), (fd // 2, "logf")), F32, lb_a=lb_a)
        zig, w_out16 = proj(xn, w_in_a, layer, fd, ((E_WIDTH, "id"), (E_WIDTH, "silu")), BF16, cast_all=w_out_a)
        y, sp = hgrn_scan(zqf, zig, onorm_a, None, None, sp, layer=layer, n_batch=bsz, rows_per_batch=rows_p,
                          row0=0, chunk=64, n_chunks=ROW_BLOCK // 64, hb=16, per_batch=False)
        y, ss = hgrn_scan(zqf, zig, onorm_a, s0_s, y, ss, layer=layer, n_batch=dec_b, rows_per_batch=dec_t,
                          row0=n_p, chunk=dec_t, n_chunks=SAMPLE_BATCHES, hb=8, per_batch=True)
        if layer + 1 < N_A:
            x, xn = matmul_residual(y, w_out16, x, (norm_a[layer + 1],), BF16)
        else:
            x, xn, xn_kv = matmul_residual(y, w_out16, x, (norm_b[0], norm_kv), BF16)
    kv_hm = kv_proj(xn_kv, w_kv)
    for layer in range(N_B):
        zb, w_out16 = proj(xn, w_in_b, layer, 0, ((E_WIDTH, "scale_hd"), (E_WIDTH, "silu")), BF16,
                           cast_all=w_out_b)
        y = swa_prompt(zb, kv_hm, sink_b[layer], bsz, nb_p)
        y = swa_sample(zb, kv_hm, sink_b[layer], y, ck, cv, row0=n_p, n_batch=dec_b, t_new=dec_t)
        if layer + 1 < N_B:
            x, xn = matmul_residual(y, w_out16, x, (norm_b[layer + 1],), BF16)
        else:
            out_p, out_t = matmul_residual(y, w_out16, x, (norm_f,), F32, keep_x=False, split_rows=bsz * seq)

    y_prompt = out_p.reshape(bsz, seq, d)
    y_sample = out_t[n_p - bsz * seq:].reshape(dec_b, dec_t, d)
    kvp = jnp.stack([kv_hm[:, (b + 1) * seq - wb:(b + 1) * seq, :HD_B] for b in range(bsz)], axis=1)
    kvp = kvp.reshape(2, KV_B, bsz, wb, HD_B)
    cache_k_prompt = jnp.transpose(kvp[0], (1, 2, 0, 3)).astype(cache_k.dtype)
    cache_v_prompt = jnp.transpose(kvp[1], (1, 2, 0, 3)).astype(cache_v.dtype)
    kvs = kv_hm[:, n_p:, :HD_B].reshape(2, KV_B, dec_b, dec_t, HD_B)
    k_new = jnp.transpose(kvs[0], (1, 2, 0, 3)).astype(cache_k.dtype)
    v_new = jnp.transpose(kvs[1], (1, 2, 0, 3)).astype(cache_v.dtype)
    cache_k_sample = jnp.concatenate([cache_k, k_new], axis=1)[:, -wb:]
    cache_v_sample = jnp.concatenate([cache_v, v_new], axis=1)[:, -wb:]

    return (y_prompt, y_sample, sp.astype(state_hgrn.dtype), cache_k_prompt, cache_v_prompt,
            ss.astype(state_hgrn.dtype), cache_k_sample, cache_v_sample)
```

```python
import functools
import math

import jax
import jax.numpy as jnp
from jax import lax
from jax.experimental import pallas as pl
from jax.experimental.pallas import tpu as pltpu

D_MODEL = 2048
E_WIDTH = 2 * D_MODEL
N_A = 2
N_B = 2
DK_A = 128
H_A = D_MODEL // DK_A
DV_A = E_WIDTH // H_A
HD_B = 64
H_B = E_WIDTH // HD_B
KV_B = H_B // 8
G_B = H_B // KV_B
WINDOW = 128
N_META = 16
EPS = 1e-6
NEG_BIG = -1e30
TINY = 1e-30

LANES = 128
BF16_SUBLANES = 16
ROW_BLOCK = 128
PAD = ROW_BLOCK - N_META
FACTOR_SAFE = 60.0
LN2 = 0.6931471805599453
VMEM_LIMIT = 56 * 1024 * 1024
KH_STEP = 8
SAMPLE_BATCHES = BF16_SUBLANES // 8
PROJ_ROWS = 1120
RES_ROWS = 448
PROJ_COLS = 1024
PROJ_SUBTILES = 5

F32 = jnp.float32
BF16 = jnp.bfloat16


def _sigmoid(x):
    return 1.0 / (1.0 + jnp.exp(-x))


def _silu(x):
    return x * _sigmoid(x)


def _params(*sem):
    return pltpu.CompilerParams(dimension_semantics=sem, vmem_limit_bytes=VMEM_LIMIT)


def _row_tile(n, cap):
    for t in range(min(cap, n) // BF16_SUBLANES * BF16_SUBLANES, 0, -BF16_SUBLANES):
        if n % t == 0:
            return t
    raise ValueError(f"no row tile for {n}")


def _prompt_block(b, i, n_batch, blocks_per_batch):
    return jnp.where(i == 0, n_batch * (blocks_per_batch - 1) + b, b * (blocks_per_batch - 1) + i - 1)


def _assemble_kernel(xp_ref, meta_ref, xs_ref, g_ref, x_ref, xn_ref, *, n_prompt_blocks, n_batch):
    r = pl.program_id(0)
    is_prompt = r < n_prompt_blocks
    is_meta = jnp.logical_not(is_prompt) & (r < n_prompt_blocks + n_batch)

    def emit(x):
        x_ref[...] = x
        ms = jnp.mean(x * x, axis=-1, keepdims=True)
        xn_ref[...] = (x * lax.rsqrt(ms + EPS) * g_ref[...]).astype(xn_ref.dtype)

    @pl.when(is_meta)
    def _():
        emit(jnp.concatenate([jnp.zeros((PAD, x_ref.shape[1]), F32), meta_ref[...].astype(F32)], axis=0))

    @pl.when(is_prompt)
    def _():
        emit(xp_ref[...].astype(F32))

    @pl.when(jnp.logical_not(is_prompt | is_meta))
    def _():
        emit(xs_ref[...].astype(F32))


def assemble(x_prompt, meta_tokens, x_sample2d, g_all, layer):
    bsz, seq, d = x_prompt.shape
    n_s = x_sample2d.shape[0]
    assert seq % ROW_BLOCK == 0 and n_s % ROW_BLOCK == 0, (seq, n_s)
    nbp = seq // ROW_BLOCK
    npb = bsz * nbp
    nsb = n_s // ROW_BLOCK
    n = (npb + bsz + nsb) * ROW_BLOCK
    kern = functools.partial(_assemble_kernel, n_prompt_blocks=npb, n_batch=bsz)
    row_spec = pl.BlockSpec((ROW_BLOCK, d), lambda r: (r, 0))
    return pl.pallas_call(
        kern,
        grid=(npb + bsz + nsb,),
        in_specs=[
            pl.BlockSpec((None, ROW_BLOCK, d), lambda r: (jnp.minimum(r // nbp, bsz - 1), r % nbp, 0)),
            pl.BlockSpec((N_META, d), lambda r: (0, 0)),
            pl.BlockSpec((ROW_BLOCK, d), lambda r: (jnp.clip(r - npb - bsz, 0, nsb - 1), 0)),
            pl.BlockSpec((None, 1, d), lambda r: (layer, 0, 0)),
        ],
        out_specs=[row_spec, row_spec],
        out_shape=[jax.ShapeDtypeStruct((n, d), F32), jax.ShapeDtypeStruct((n, d), BF16)],
        compiler_params=_params("arbitrary"),
        name="assemble",
    )(x_prompt, meta_tokens, x_sample2d, g_all.reshape(g_all.shape[0], 1, d))


def _lower_bound(lbp, layer):
    e = jnp.exp(lbp - jnp.max(lbp, axis=0, keepdims=True))
    p = e / jnp.sum(e, axis=0, keepdims=True)
    acc = p[0:1]
    for j in range(1, layer + 1):
        acc = acc + p[j:j + 1]
    return acc - p[0:1]


def _log_forget(f, lb, case):
    ef = jnp.exp(-jnp.abs(f))
    r = 1.0 / (1.0 + ef)
    if case == "none":
        return jnp.log(r) + (jnp.log1p(-lb) + jnp.minimum(f, 0.0))
    sig = jnp.where(f >= 0, 1.0, ef) * r
    arg_pos = jnp.maximum(lb, TINY) + (1.0 - lb) * sig
    if case == "all":
        return jnp.log(arg_pos)
    pos = lb > 0
    arg = jnp.where(pos, arg_pos, r)
    off = jnp.where(pos, 0.0, jnp.log1p(-lb) + jnp.minimum(f, 0.0))
    return jnp.log(arg) + off


def _proj_kernel(*refs, segments, layer, with_cast):
    refs = list(refs)
    x_ref, w_ref = refs[:2]
    pos = 2
    lb_ref = cast_src_ref = cast_dst_ref = None
    if any(mode == "logf" for _, _, mode in segments):
        lb_ref = refs[pos]
        pos += 1
    if with_cast:
        cast_src_ref = refs[pos]
        pos += 1
    o_ref = refs[pos]
    pos += 1
    if with_cast:
        cast_dst_ref = refs[pos]
        pos += 1
    wb_ref = refs[pos]
    j = pl.program_id(0)

    if with_cast:
        cast_dst_ref[...] = cast_src_ref[...].astype(BF16)

    @pl.when(pl.program_id(1) == 0)
    def _():
        wb_ref[...] = w_ref[...].astype(BF16)

    def tile(act):
        tm = x_ref.shape[0]
        nsub = PROJ_SUBTILES * max(tm // PROJ_ROWS, 1)
        sub = tm // nsub if tm % (nsub * BF16_SUBLANES) == 0 else tm
        for r0 in range(0, tm, sub):
            acc = jnp.dot(x_ref[r0:r0 + sub, :], wb_ref[...], preferred_element_type=F32)
            o_ref[r0:r0 + sub, :] = act(acc).astype(o_ref.dtype)

    acts = {
        "id": lambda a: a,
        "silu": _silu,
        "silu_dk": lambda a: _silu(a) * (DK_A ** -0.5),
        "scale_hd": lambda a: a * (HD_B ** -0.5),
    }
    for j0, j1, mode in segments:
        @pl.when((j >= j0) & (j < j1))
        def _(mode=mode):
            if mode != "logf":
                tile(acts[mode])
                return
            lb = _lower_bound(lb_ref[...], layer)
            n_pos = jnp.sum((lb > 0).astype(jnp.int32))
            cases = (("all", n_pos == lb.shape[1]), ("none", n_pos == 0),
                     ("mixed", (n_pos > 0) & (n_pos < lb.shape[1])))
            for case, cond in cases:
                @pl.when(cond)
                def _(case=case):
                    tile(lambda a: _log_forget(a, lb, case))


def proj(xn, w_all, layer, segments, out_dtype, lb_a=None, cast_all=None):
    n, d = xn.shape
    tn = PROJ_COLS
    tm = _row_tile(n, PROJ_ROWS * (4 // jnp.dtype(out_dtype).itemsize))
    bounds, starts, j0 = [], [], 0
    for col0, ncols_seg, mode in segments:
        bounds.append((j0, j0 + ncols_seg // tn, mode))
        starts.append(col0 // tn - j0)
        j0 += ncols_seg // tn
    ncols = j0 * tn

    def w_tile(j):
        t = j + starts[0]
        for (ja, _, _), st in zip(bounds[1:], starts[1:]):
            t = jnp.where(j >= ja, j + st, t)
        return t

    in_specs = [
        pl.BlockSpec((tm, d), lambda j, i: (i, 0)),
        pl.BlockSpec((None, d, tn), lambda j, i: (layer, 0, w_tile(j))),
    ]
    args = [xn, w_all]
    for ja, jz, mode in bounds:
        if mode == "logf":
            in_specs.append(pl.BlockSpec((lb_a.shape[0], tn),
                                         lambda j, i, ja=ja, jz=jz: (0, jnp.clip(j - ja, 0, jz - ja - 1))))
            args.append(lb_a)
    nj, ni = ncols // tn, n // tm
    out_specs = [pl.BlockSpec((tm, tn), lambda j, i: (i, j))]
    out_shape = [jax.ShapeDtypeStruct((n, ncols), out_dtype)]
    if cast_all is not None:
        _, ce, cd = cast_all.shape
        slab = ce // (nj * ni)
        assert slab * nj * ni == ce and slab % BF16_SUBLANES == 0, (ce, nj, ni)
        in_specs.append(pl.BlockSpec((None, slab, cd), lambda j, i: (layer, j * ni + i, 0)))
        args.append(cast_all)
        out_specs.append(pl.BlockSpec((slab, cd), lambda j, i: (j * ni + i, 0)))
        out_shape.append(jax.ShapeDtypeStruct((ce, cd), BF16))
    kern = functools.partial(_proj_kernel, segments=tuple(bounds), layer=layer, with_cast=cast_all is not None)
    res = pl.pallas_call(
        kern,
        grid=(nj, ni),
        in_specs=in_specs,
        out_specs=out_specs,
        out_shape=out_shape,
        scratch_shapes=[pltpu.VMEM((d, tn), BF16)],
        compiler_params=_params("arbitrary", "arbitrary"),
        name="proj",
    )(*args)
    return res if cast_all is not None else res[0]


def _kv_proj_kernel(x_ref, w_ref, o_ref, wb_ref):
    @pl.when(pl.program_id(0) == 0)
    def _():
        wb_ref[...] = w_ref[...].astype(BF16)

    kv = jnp.dot(x_ref[...], wb_ref[...], preferred_element_type=F32)
    ones = jnp.ones((kv.shape[0], HD_B), F32)
    for j in range(KV_B):
        k_j = kv[:, j * HD_B:(j + 1) * HD_B]
        o_ref[j] = jnp.concatenate([k_j, k_j], axis=1)
        v_j = kv[:, (KV_B + j) * HD_B:(KV_B + j + 1) * HD_B]
        o_ref[KV_B + j] = jnp.concatenate([v_j, ones], axis=1)


def kv_proj(xn, w):
    n, d = xn.shape
    tm = _row_tile(n, RES_ROWS)
    return pl.pallas_call(
        _kv_proj_kernel,
        grid=(n // tm,),
        in_specs=[
            pl.BlockSpec((tm, d), lambda i: (i, 0)),
            pl.BlockSpec((d, 2 * KV_B * HD_B), lambda i: (0, 0), pipeline_mode=pl.Buffered(1)),
        ],
        out_specs=pl.BlockSpec((2 * KV_B, tm, 2 * HD_B), lambda i: (0, i, 0)),
        out_shape=jax.ShapeDtypeStruct((2 * KV_B, n, 2 * HD_B), F32),
        scratch_shapes=[pltpu.VMEM((d, 2 * KV_B * HD_B), BF16)],
        compiler_params=_params("arbitrary"),
        name="kv_proj",
    )(xn, w)


def _matmul_res_kernel(y_ref, w_ref, x_ref, g_ref, *out_refs, n_norm, keep_x, head_tiles):
    x = x_ref[...] + jnp.dot(y_ref[...], w_ref[...], preferred_element_type=F32)
    ms = jnp.mean(x * x, axis=-1, keepdims=True)
    xh = x * lax.rsqrt(ms + EPS)
    if head_tiles is not None:
        head_ref, tail_ref = out_refs
        i = pl.program_id(0)

        @pl.when(i < head_tiles)
        def _():
            head_ref[...] = xh * g_ref[0]

        @pl.when(i >= head_tiles)
        def _():
            tail_ref[...] = xh * g_ref[0]
        return
    pos = 0
    if keep_x:
        out_refs[0][...] = x
        pos = 1
    for t in range(n_norm):
        out_refs[pos + t][...] = (xh * g_ref[t]).astype(out_refs[pos + t].dtype)


def matmul_residual(y, w, x, gains, norm_dtype, keep_x=True, split_rows=None):
    n, e = y.shape
    d = w.shape[1]
    gs = jnp.stack([g.reshape(1, d) for g in gains])
    if split_rows is None:
        tm = _row_tile(n, RES_ROWS)
        head_tiles = None
        row_spec = pl.BlockSpec((tm, d), lambda i: (i, 0))
        out_shapes = ([jax.ShapeDtypeStruct((n, d), F32)] if keep_x else []) + \
            [jax.ShapeDtypeStruct((n, d), norm_dtype) for _ in gains]
        out_specs = [row_spec] * len(out_shapes)
    else:
        assert len(gains) == 1 and not keep_x and norm_dtype == F32
        tm = _row_tile(math.gcd(split_rows, n - split_rows), RES_ROWS)
        head_tiles = split_rows // tm
        row_spec = pl.BlockSpec((tm, d), lambda i: (i, 0))
        out_shapes = [jax.ShapeDtypeStruct((split_rows, d), F32), jax.ShapeDtypeStruct((n - split_rows, d), F32)]
        out_specs = [pl.BlockSpec((tm, d), lambda i: (jnp.minimum(i, head_tiles - 1), 0)),
                     pl.BlockSpec((tm, d), lambda i: (jnp.maximum(i - head_tiles, 0), 0))]
    kern = functools.partial(_matmul_res_kernel, n_norm=len(gains), keep_x=keep_x, head_tiles=head_tiles)
    return pl.pallas_call(
        kern,
        grid=(n // tm,),
        in_specs=[
            pl.BlockSpec((tm, e), lambda i: (i, 0)),
            pl.BlockSpec((e, d), lambda i: (0, 0), pipeline_mode=pl.Buffered(1)),
            row_spec,
            pl.BlockSpec((len(gains), 1, d), lambda i: (0, 0, 0)),
        ],
        out_specs=out_specs,
        out_shape=out_shapes,
        compiler_params=_params("arbitrary"),
        name="matmul_residual",
    )(y, w, x, gs)


def _hgrn_kernel(*refs, chunk, n_chunks, hb, per_batch, zero_init, n_alias):
    refs = list(refs)
    zq_ref, zf_ref, zi_ref, zg_ref, on_ref = refs[:5]
    pos = 5
    s0_ref = None
    if not zero_init:
        s0_ref = refs[pos]
        pos += 1
    pos += n_alias
    y_ref, sfin_ref, s_ref, cum_sc, oint_sc, ystage_sc, flag_ref = refs[pos:]
    c = chunk
    t = pl.program_id(2)
    aligned = c % BF16_SUBLANES == 0

    @pl.when(t == 0)
    def _():
        if zero_init:
            s_ref[...] = jnp.zeros_like(s_ref)
        else:
            s_ref[...] = s0_ref[...]

    on_g = on_ref[...]
    if aligned:
        zq_all = zi_all = zg_all = None
    else:
        zq_all = zq_ref[...].astype(F32)
        zi_all = zi_ref[...].astype(F32)
        zg_all = zg_ref[...].astype(F32)

    ri = lax.broadcasted_iota(jnp.int32, (c, c), 0)
    ci_ = lax.broadcasted_iota(jnp.int32, (c, c), 1)
    tril = (ci_ <= ri).astype(BF16)
    row_c = lax.broadcasted_iota(jnp.int32, (c, LANES), 0)
    col_c = lax.broadcasted_iota(jnp.int32, (c, LANES), 1)
    causal = col_c <= row_c
    zpad_k = jnp.zeros((LANES - c, DK_A), F32)
    zpad_x = jnp.zeros((LANES - c - 8, DK_A), F32)

    def queries(rows, hs):
        return zq_ref[rows, hs].astype(F32) if aligned else zq_all[rows, hs]

    def values(rows, vs_):
        return zi_ref[rows, vs_].astype(F32) if aligned else zi_all[rows, vs_]

    def values_padded(rows, vs_):
        return jnp.concatenate([zi_all[rows, vs_], jnp.zeros((LANES - c, DV_A), F32)], axis=0).astype(BF16)

    def emit_y(o, rows, vs_):
        ms = jnp.mean(o * o, axis=-1, keepdims=True)
        o = o * lax.rsqrt(ms + EPS) * on_g
        if aligned:
            y_ref[rows, vs_] = (o * zg_ref[rows, vs_].astype(F32)).astype(y_ref.dtype)
        else:
            ystage_sc[rows, vs_] = o * zg_all[rows, vs_]

    for ci in range(n_chunks):
        rows = slice(ci * c, (ci + 1) * c)
        si = ci if per_batch else 0
        logf = zf_ref[rows, :]
        h1 = logf.astype(BF16)
        h2 = (logf - h1.astype(F32)).astype(BF16)
        cum = jnp.dot(tril, h1, preferred_element_type=F32) + jnp.dot(tril, h2, preferred_element_type=F32)
        last = cum[c - 1:c, :]
        mid = cum[c // 2 - 1:c // 2, :]
        dev = jnp.max(jnp.maximum(cum[0:1, :] - mid, mid - last))
        flag_ref[ci] = (dev <= FACTOR_SAFE).astype(jnp.int32)
        cum_sc[rows, :] = cum

        for h in range(hb):
            hs = slice(h * DK_A, (h + 1) * DK_A)
            vs_ = slice(h * DV_A, (h + 1) * DV_A)
            q_h = queries(rows, hs)
            k_h = 1.0 - jnp.exp(zf_ref[rows, hs])
            cum_h = cum_sc[rows, hs]
            d_h = cum_h - mid[:, hs]
            qm = (q_h * jnp.exp(d_h)).astype(BF16)
            km = k_h * jnp.exp(-d_h)
            s_prev = s_ref[si, h]
            qa = (q_h * jnp.exp(cum_h)).astype(BF16)
            o_inter = jnp.dot(qa, s_prev.astype(BF16), preferred_element_type=F32)
            oint_sc[rows, vs_] = o_inter
            kd = k_h * jnp.exp(last[:, hs] - cum_h)
            el8 = jnp.broadcast_to(jnp.exp(last[:, hs]), (8, DK_A))
            x_t = jnp.concatenate([kd, el8, zpad_x], axis=0).T
            if aligned:
                att = lax.dot_general(qm, km.astype(BF16), (((1,), (1,)), ((), ())), preferred_element_type=F32)
                att = jnp.where(causal[:, :c], att, 0.0).astype(BF16)
                lhs = jnp.concatenate([att, x_t[:, :c].astype(BF16)], axis=0)
                res = jnp.dot(lhs, zi_ref[rows, vs_], preferred_element_type=F32)
                o_intra, ds = res[:c], res[c:]
            else:
                v_pad = values_padded(rows, vs_)
                km_pad = jnp.concatenate([km, zpad_k], axis=0).astype(BF16)
                att = lax.dot_general(qm, km_pad, (((1,), (1,)), ((), ())), preferred_element_type=F32)
                att = jnp.where(causal, att, 0.0).astype(BF16)
                o_intra = jnp.dot(att, v_pad, preferred_element_type=F32)
                ds = jnp.dot(x_t.astype(BF16), v_pad, preferred_element_type=F32)
            s_ref[si, h] = x_t[:, c:c + 1] * s_prev + ds
            emit_y(o_inter + o_intra, rows, vs_)

    for ci in range(n_chunks):
        rows = slice(ci * c, (ci + 1) * c)

        @pl.when(flag_ref[ci] == 0)
        def _(rows=rows):
            row_v = lax.broadcasted_iota(jnp.int32, (c, 1), 0)

            def row_of(a, s):
                return jnp.sum(jnp.where(row_v == s, a, 0.0), axis=0, keepdims=True)

            for h in range(hb):
                hs = slice(h * DK_A, (h + 1) * DK_A)
                vs_ = slice(h * DV_A, (h + 1) * DV_A)
                q_h = queries(rows, hs)
                k_h = 1.0 - jnp.exp(zf_ref[rows, hs])
                cum_h = cum_sc[rows, hs]
                v_h = values(rows, vs_)

                def body(s, acc_o, q_h=q_h, k_h=k_h, cum_h=cum_h, v_h=v_h):
                    dec = jnp.exp(jnp.minimum(cum_h - row_of(cum_h, s), 0.0))
                    w = jnp.sum(q_h * row_of(k_h, s) * dec, axis=-1, keepdims=True)
                    w = jnp.where(row_v >= s, w, 0.0)
                    return acc_o + w * row_of(v_h, s)

                o_intra = lax.fori_loop(0, c, body, jnp.zeros((c, DV_A), F32))
                emit_y(oint_sc[rows, vs_] + o_intra, rows, vs_)

    if not aligned:
        y_ref[...] = ystage_sc[...].astype(y_ref.dtype)

    @pl.when(t == pl.num_programs(2) - 1)
    def _():
        sfin_ref[...] = s_ref[...]


def hgrn_scan(zf, zigq, onorm_a, s0_all, y_prev, states_prev, *, layer, n_batch, rows_per_batch, row0, chunk,
              n_chunks, hb, per_batch):
    n = zf.shape[0]
    tb = chunk * n_chunks
    ns = n_chunks if per_batch else 1
    nt = 1 if per_batch else rows_per_batch // tb
    nbg = n_batch // ns
    ng = H_A // hb
    wk, wv = hb * DK_A, hb * DV_A
    q_off = (2 * E_WIDTH) // wk
    g_off = E_WIDTH // wv
    if per_batch:
        rb0 = row0 // tb
        row = lambda b, g, t: rb0 + b
    else:
        assert tb == ROW_BLOCK and row0 == 0
        row = lambda b, g, t: _prompt_block(b, t, n_batch, nt)
    in_specs = [
        pl.BlockSpec((tb, wk), lambda b, g, t: (row(b, g, t), q_off + g)),
        pl.BlockSpec((tb, wk), lambda b, g, t: (row(b, g, t), g)),
        pl.BlockSpec((tb, wv), lambda b, g, t: (row(b, g, t), g)),
        pl.BlockSpec((tb, wv), lambda b, g, t: (row(b, g, t), g_off + g)),
        pl.BlockSpec((None, 1, DV_A), lambda b, g, t: (layer, 0, 0)),
    ]
    args = [zigq, zf, zigq, zigq, onorm_a.reshape(N_A, 1, DV_A)]
    state_spec = pl.BlockSpec((None, ns, hb, DK_A, DV_A), lambda b, g, t: (layer, b, g, 0, 0))
    if s0_all is not None:
        in_specs.append(state_spec)
        args.append(s0_all)
    aliases = {}
    for out_idx, prev in ((0, y_prev), (1, states_prev)):
        if prev is not None:
            aliases[len(args)] = out_idx
            in_specs.append(pl.BlockSpec(memory_space=pl.ANY))
            args.append(prev)
    kern = functools.partial(_hgrn_kernel, chunk=chunk, n_chunks=n_chunks, hb=hb, per_batch=per_batch,
                             zero_init=s0_all is None, n_alias=len(aliases))
    return pl.pallas_call(
        kern,
        grid=(nbg, ng, nt),
        in_specs=in_specs,
        out_specs=[
            pl.BlockSpec((tb, wv), lambda b, g, t: (row(b, g, t), g)),
            state_spec,
        ],
        out_shape=[
            jax.ShapeDtypeStruct((n, E_WIDTH), BF16),
            jax.ShapeDtypeStruct((N_A, n_batch, H_A, DK_A, DV_A), F32),
        ],
        scratch_shapes=[
            pltpu.VMEM((ns, hb, DK_A, DV_A), F32),
            pltpu.VMEM((tb, wk), F32),
            pltpu.VMEM((tb, wv), F32),
            pltpu.VMEM((tb, wv), F32),
            pltpu.SMEM((n_chunks,), jnp.int32),
        ],
        input_output_aliases=aliases,
        compiler_params=_params("arbitrary", "arbitrary", "arbitrary"),
        name=f"hgrn_scan_c{chunk}",
    )(*args)


def _swa_prompt_kernel(sink_ref, q_ref, g_ref, kc_ref, kp_ref, vc_ref, vp_ref, y_ref, d_sc, s_sc, p_sc):
    blk = pl.program_id(1)
    kg = pl.program_id(2)
    w = WINDOW
    npair = G_B // 2

    @pl.when(kg == 0)
    def _():
        row = lax.broadcasted_iota(jnp.int32, (w, 2 * w), 0)
        col = lax.broadcasted_iota(jnp.int32, (w, 2 * w), 1)
        dist = w + row - col
        key_pos = (blk - 1) * w + col
        valid = (dist >= 0) & (dist < w) & (key_pos >= PAD)
        d_sc[...] = jnp.where(valid, dist.astype(F32), -NEG_BIG)

    lane = lax.broadcasted_iota(jnp.int32, (w, LANES), 1)
    lane_k = lax.broadcasted_iota(jnp.int32, (2 * w, LANES), 1)
    row_k = lax.broadcasted_iota(jnp.int32, (2 * w, LANES), 0)
    for kk in range(KH_STEP):
        kh = kg * KH_STEP + kk
        kdup = jnp.concatenate([kp_ref[kk], kc_ref[kk]], axis=0)
        k_par = [jnp.where(lane_k < HD_B, kdup, 0.0).astype(BF16), jnp.where(lane_k < HD_B, 0.0, kdup).astype(BF16)]
        vraw = jnp.concatenate([vp_ref[kk], vc_ref[kk]], axis=0)
        vext = jnp.where(row_k == 0, jnp.where(lane_k < HD_B, 0.0, vraw), vraw).astype(BF16)
        qs = jnp.concatenate(
            [q_ref[:, (kk * npair + j) * LANES:(kk * npair + j + 1) * LANES] for j in range(npair)],
            axis=0).astype(BF16)
        for par in range(2):
            s_sc[kk, par] = lax.dot_general(qs, k_par[par], (((1,), (1,)), ((), ())),
                                            preferred_element_type=F32)
        for j in range(npair):
            rows = slice(j * w, (j + 1) * w)
            for par in range(2):
                head = kh * G_B + 2 * j + par
                slope = jnp.exp(-LN2 * 8.0 * (head + 1).astype(F32) / H_B)
                sink = sink_ref[head]
                s_lo = s_sc[kk, par, rows, :w] - slope * d_sc[:, :w]
                s_lo = jnp.where(lane == 0, sink, s_lo)
                s_hi = s_sc[kk, par, rows, w:] - slope * d_sc[:, w:]
                m = jnp.max(jnp.maximum(s_lo, s_hi), axis=-1, keepdims=True)
                p_sc[kk, par, rows, :w] = jnp.exp(s_lo - m).astype(BF16)
                p_sc[kk, par, rows, w:] = jnp.exp(s_hi - m).astype(BF16)
        o_par = [jnp.dot(p_sc[kk, par], vext, preferred_element_type=F32) for par in range(2)]
        for j in range(npair):
            o_even = o_par[0][j * w:(j + 1) * w, :]
            o_odd = o_par[1][j * w:(j + 1) * w, :]
            num = jnp.where(lane < HD_B, o_even, pltpu.roll(o_odd, HD_B, 1))
            den = jnp.where(lane < HD_B, pltpu.roll(o_even, HD_B, 1), o_odd)
            cs = slice((kk * npair + j) * LANES, (kk * npair + j + 1) * LANES)
            y_ref[:, cs] = (num / den * g_ref[:, cs].astype(F32)).astype(BF16)


def swa_prompt(zb, kv_hm, sink, n_batch, blocks_per_batch):
    n = zb.shape[0]
    w = WINDOW
    gw = KH_STEP * G_B * HD_B
    nb = blocks_per_batch
    nkg = KV_B // KH_STEP
    cur = lambda b, i, kg: _prompt_block(b, i, n_batch, nb)
    prev = lambda b, i, kg: _prompt_block(b, jnp.maximum(i - 1, 0), n_batch, nb)
    kv_block = (KH_STEP, w, 2 * HD_B)
    return pl.pallas_call(
        _swa_prompt_kernel,
        grid=(n_batch, nb, nkg),
        in_specs=[
            pl.BlockSpec(memory_space=pltpu.SMEM),
            pl.BlockSpec((w, gw), lambda b, i, kg: (cur(b, i, kg), kg)),
            pl.BlockSpec((w, gw), lambda b, i, kg: (cur(b, i, kg), nkg + kg)),
            pl.BlockSpec(kv_block, lambda b, i, kg: (kg, cur(b, i, kg), 0)),
            pl.BlockSpec(kv_block, lambda b, i, kg: (kg, prev(b, i, kg), 0)),
            pl.BlockSpec(kv_block, lambda b, i, kg: (nkg + kg, cur(b, i, kg), 0)),
            pl.BlockSpec(kv_block, lambda b, i, kg: (nkg + kg, prev(b, i, kg), 0)),
        ],
        out_specs=pl.BlockSpec((w, gw), lambda b, i, kg: (cur(b, i, kg), kg)),
        out_shape=jax.ShapeDtypeStruct((n, E_WIDTH), BF16),
        scratch_shapes=[
            pltpu.VMEM((w, 2 * w), F32),
            pltpu.VMEM((KH_STEP, 2, G_B // 2 * w, 2 * w), F32),
            pltpu.VMEM((KH_STEP, 2, G_B // 2 * w, 2 * w), BF16),
        ],
        compiler_params=_params("arbitrary", "arbitrary", "arbitrary"),
        name="swa_prompt",
    )(sink, zb, zb, kv_hm, kv_hm, kv_hm, kv_hm)


def _swa_sample_kernel(sink_ref, q_ref, g_ref, kn_ref, vn_ref, ck_ref, cv_ref, yprev_ref, y_ref, *, t_new, nbs):
    del yprev_ref
    wb = ck_ref.shape[1]
    nk = wb + t_new
    nr = G_B * t_new
    row = lax.broadcasted_iota(jnp.int32, (nr, nk), 0)
    col = lax.broadcasted_iota(jnp.int32, (nr, nk), 1)
    tok = row % t_new
    grp = row // t_new
    dist = wb + tok - col
    valid = (dist >= 0) & (dist < WINDOW)
    distf = dist.astype(F32)
    grp1 = lax.broadcasted_iota(jnp.int32, (nr, 1), 0) // t_new
    q_all = q_ref[...].astype(F32)
    g_all = g_ref[...].astype(F32)
    for kh in range(KV_B):
        ks = slice(kh * HD_B, (kh + 1) * HD_B)
        slope = jnp.exp(-LN2 * 8.0 * (kh * G_B + grp + 1).astype(F32) / H_B)
        sink = jnp.zeros((nr, 1), F32)
        for g in range(G_B):
            sink = jnp.where(grp1 == g, sink_ref[kh * G_B + g], sink)
        outs = []
        for bi in range(nbs):
            rb = slice(bi * t_new, (bi + 1) * t_new)
            kf = jnp.concatenate([ck_ref[bi, :, ks], kn_ref[kh][rb, :HD_B]], axis=0).astype(BF16)
            vf = jnp.concatenate([cv_ref[bi, :, ks], vn_ref[kh][rb, :HD_B]], axis=0).astype(BF16)
            qs = jnp.concatenate(
                [q_all[rb, (kh * G_B + g) * HD_B:(kh * G_B + g + 1) * HD_B] for g in range(G_B)], axis=0)
            s = lax.dot_general(qs.astype(BF16), kf, (((1,), (1,)), ((), ())), preferred_element_type=F32)
            s = s - slope * distf
            s = jnp.where(valid, s, NEG_BIG)
            m = jnp.maximum(jnp.max(s, axis=-1, keepdims=True), sink)
            p = jnp.where(valid, jnp.exp(s - m), 0.0)
            den = jnp.sum(p, axis=-1, keepdims=True) + jnp.exp(sink - m)
            outs.append(jnp.dot(p.astype(BF16), vf, preferred_element_type=F32) / den)
        for g in range(G_B):
            hsl = slice((kh * G_B + g) * HD_B, (kh * G_B + g + 1) * HD_B)
            o = jnp.concatenate([outs[bi][g * t_new:(g + 1) * t_new, :] for bi in range(nbs)], axis=0)
            y_ref[:, hsl] = (o * g_all[:, hsl]).astype(y_ref.dtype)


def swa_sample(zb, kv_hm, sink, y_prev, cache_k, cache_v, *, row0, n_batch, t_new):
    n = zb.shape[0]
    wb = cache_k.shape[1]
    nbs = SAMPLE_BATCHES
    tb = nbs * t_new
    rb0 = row0 // tb
    kern = functools.partial(_swa_sample_kernel, t_new=t_new, nbs=nbs)
    return pl.pallas_call(
        kern,
        grid=(n_batch // nbs,),
        in_specs=[
            pl.BlockSpec(memory_space=pltpu.SMEM),
            pl.BlockSpec((tb, E_WIDTH), lambda b: (rb0 + b, 0)),
            pl.BlockSpec((tb, E_WIDTH), lambda b: (rb0 + b, 1)),
            pl.BlockSpec((KV_B, tb, 2 * HD_B), lambda b: (0, rb0 + b, 0)),
            pl.BlockSpec((KV_B, tb, 2 * HD_B), lambda b: (1, rb0 + b, 0)),
            pl.BlockSpec((nbs, wb, KV_B * HD_B), lambda b: (b, 0, 0)),
            pl.BlockSpec((nbs, wb, KV_B * HD_B), lambda b: (b, 0, 0)),
            pl.BlockSpec(memory_space=pl.ANY),
        ],
        out_specs=pl.BlockSpec((tb, E_WIDTH), lambda b: (rb0 + b, 0)),
        out_shape=jax.ShapeDtypeStruct((n, E_WIDTH), BF16),
        input_output_aliases={7: 0},
        compiler_params=_params("arbitrary"),
        name="swa_sample",
    )(sink, zb, zb, kv_hm, kv_hm, cache_k, cache_v, y_prev)


def kernel(x_prompt, x_sample, state_hgrn, cache_k, cache_v, meta_tokens, norm_a, w_in_a, lb_a, onorm_a,
           w_out_a, norm_kv, w_kv, norm_b, w_in_b, sink_b, w_out_b, norm_f):
    bsz, seq, d = x_prompt.shape
    dec_b, dec_t, _ = x_sample.shape
    wb = cache_k.shape[1]

    rows_p = PAD + N_META + seq
    nb_p = rows_p // ROW_BLOCK
    n_p = bsz * rows_p
    n_s = dec_b * dec_t
    fd = 2 * H_A * DK_A
    s0_s = state_hgrn.astype(F32)
    ck = cache_k.reshape(dec_b, wb, KV_B * HD_B)
    cv = cache_v.reshape(dec_b, wb, KV_B * HD_B)

    x, xn = assemble(x_prompt, meta_tokens, x_sample.reshape(n_s, d), norm_a, 0)
    sp = ss = None
    for layer in range(N_A):
        zf, w_out16 = proj(xn, w_in_a, layer, ((fd // 2, fd // 2, "logf"),), F32, lb_a=lb_a, cast_all=w_out_a)
        zigq = proj(xn, w_in_a, layer, ((fd, E_WIDTH, "id"), (fd + E_WIDTH, E_WIDTH, "silu"),
                                        (0, fd // 2, "silu_dk")), BF16)
        y, sp = hgrn_scan(zf, zigq, onorm_a, None, None, sp, layer=layer, n_batch=bsz, rows_per_batch=rows_p,
                          row0=0, chunk=64, n_chunks=ROW_BLOCK // 64, hb=16, per_batch=False)
        y, ss = hgrn_scan(zf, zigq, onorm_a, s0_s, y, ss, layer=layer, n_batch=dec_b, rows_per_batch=dec_t,
                          row0=n_p, chunk=dec_t, n_chunks=SAMPLE_BATCHES, hb=8, per_batch=True)
        if layer + 1 < N_A:
            x, xn = matmul_residual(y, w_out16, x, (norm_a[layer + 1],), BF16)
        else:
            x, xn, xn_kv = matmul_residual(y, w_out16, x, (norm_b[0], norm_kv), BF16)
    kv_hm = kv_proj(xn_kv, w_kv)
    for layer in range(N_B):
        zb, w_out16 = proj(xn, w_in_b, layer, ((0, E_WIDTH, "scale_hd"), (E_WIDTH, E_WIDTH, "silu")), BF16,
                           cast_all=w_out_b)
        y = swa_prompt(zb, kv_hm, sink_b[layer], bsz, nb_p)
        y = swa_sample(zb, kv_hm, sink_b[layer], y, ck, cv, row0=n_p, n_batch=dec_b, t_new=dec_t)
        if layer + 1 < N_B:
            x, xn = matmul_residual(y, w_out16, x, (norm_b[layer + 1],), BF16)
        else:
            out_p, out_t = matmul_residual(y, w_out16, x, (norm_f,), F32, keep_x=False, split_rows=bsz * seq)

    y_prompt = out_p.reshape(bsz, seq, d)
    y_sample = out_t[n_p - bsz * seq:].reshape(dec_b, dec_t, d)
    kvp = jnp.stack([kv_hm[:, (b + 1) * seq - wb:(b + 1) * seq, :HD_B] for b in range(bsz)], axis=1)
    kvp = kvp.reshape(2, KV_B, bsz, wb, HD_B)
    cache_k_prompt = jnp.transpose(kvp[0], (1, 2, 0, 3)).astype(cache_k.dtype)
    cache_v_prompt = jnp.transpose(kvp[1], (1, 2, 0, 3)).astype(cache_v.dtype)
    kvs = kv_hm[:, n_p:, :HD_B].reshape(2, KV_B, dec_b, dec_t, HD_B)
    k_new = jnp.transpose(kvs[0], (1, 2, 0, 3)).astype(cache_k.dtype)
    v_new = jnp.transpose(kvs[1], (1, 2, 0, 3)).astype(cache_v.dtype)
    cache_k_sample = jnp.concatenate([cache_k, k_new], axis=1)[:, -wb:]
    cache_v_sample = jnp.concatenate([cache_v, v_new], axis=1)[:, -wb:]

    return (y_prompt, y_sample, sp.astype(state_hgrn.dtype), cache_k_prompt, cache_v_prompt,
            ss.astype(state_hgrn.dtype), cache_k_sample, cache_v_sample)
```

```python
import functools
import math

import jax
import jax.numpy as jnp
from jax import lax
from jax.experimental import pallas as pl
from jax.experimental.pallas import tpu as pltpu

D_MODEL = 2048
E_WIDTH = 2 * D_MODEL
N_A = 2
N_B = 2
DK_A = 128
H_A = D_MODEL // DK_A
DV_A = E_WIDTH // H_A
HD_B = 64
H_B = E_WIDTH // HD_B
KV_B = H_B // 8
G_B = H_B // KV_B
WINDOW = 128
N_META = 16
EPS = 1e-6
NEG_BIG = -1e30
TINY = 1e-30

LANES = 128
SUBLANES = 8
BF16_SUBLANES = 16
ROW_BLOCK = 128
PAD = ROW_BLOCK - N_META
FACTOR_SAFE = 60.0
LN2 = 0.6931471805599453
VMEM_LIMIT = 56 * 1024 * 1024
KH_STEP = 8
HGRN_CHUNK = 64
HGRN_HEADS_PROMPT = 16
HGRN_HEADS_SAMPLE = 8
PROJ_ROWS = 1120
RES_ROWS = 448
PROJ_COLS = 1024
PROJ_SUBTILES = 5

F32 = jnp.float32
BF16 = jnp.bfloat16


def _sigmoid(x):
    return 1.0 / (1.0 + jnp.exp(-x))


def _silu(x):
    return x * _sigmoid(x)


def _params(*sem):
    return pltpu.CompilerParams(dimension_semantics=sem, vmem_limit_bytes=VMEM_LIMIT)


def _row_tile(n, cap):
    for t in range(min(cap, n) // BF16_SUBLANES * BF16_SUBLANES, 0, -BF16_SUBLANES):
        if n % t == 0:
            return t
    raise ValueError(f"no row tile for {n}")


def _prompt_block(b, i, n_batch, blocks_per_batch):
    return jnp.where(i == 0, n_batch * (blocks_per_batch - 1) + b, b * (blocks_per_batch - 1) + i - 1)


def _assemble_kernel(xp_ref, meta_ref, xs_ref, g_ref, x_ref, xn_ref, *, n_prompt_blocks, n_batch):
    r = pl.program_id(0)
    is_prompt = r < n_prompt_blocks
    is_meta = jnp.logical_not(is_prompt) & (r < n_prompt_blocks + n_batch)

    def emit(x):
        x_ref[...] = x
        ms = jnp.mean(x * x, axis=-1, keepdims=True)
        xn_ref[...] = (x * lax.rsqrt(ms + EPS) * g_ref[...]).astype(xn_ref.dtype)

    @pl.when(is_meta)
    def _():
        emit(jnp.concatenate([jnp.zeros((PAD, x_ref.shape[1]), F32), meta_ref[...].astype(F32)], axis=0))

    @pl.when(is_prompt)
    def _():
        emit(xp_ref[...].astype(F32))

    @pl.when(jnp.logical_not(is_prompt | is_meta))
    def _():
        emit(xs_ref[...].astype(F32))


def assemble(x_prompt, meta_tokens, x_sample2d, g_all, layer):
    bsz, seq, d = x_prompt.shape
    n_s = x_sample2d.shape[0]
    assert seq % ROW_BLOCK == 0 and n_s % ROW_BLOCK == 0, (seq, n_s)
    nbp = seq // ROW_BLOCK
    npb = bsz * nbp
    nsb = n_s // ROW_BLOCK
    n = (npb + bsz + nsb) * ROW_BLOCK
    kern = functools.partial(_assemble_kernel, n_prompt_blocks=npb, n_batch=bsz)
    row_spec = pl.BlockSpec((ROW_BLOCK, d), lambda r: (r, 0))
    return pl.pallas_call(
        kern,
        grid=(npb + bsz + nsb,),
        in_specs=[
            pl.BlockSpec((None, ROW_BLOCK, d), lambda r: (jnp.minimum(r // nbp, bsz - 1), r % nbp, 0)),
            pl.BlockSpec((N_META, d), lambda r: (0, 0)),
            pl.BlockSpec((ROW_BLOCK, d), lambda r: (jnp.clip(r - npb - bsz, 0, nsb - 1), 0)),
            pl.BlockSpec((None, 1, d), lambda r: (layer, 0, 0)),
        ],
        out_specs=[row_spec, row_spec],
        out_shape=[jax.ShapeDtypeStruct((n, d), F32), jax.ShapeDtypeStruct((n, d), BF16)],
        compiler_params=_params("arbitrary"),
        name="assemble",
    )(x_prompt, meta_tokens, x_sample2d, g_all.reshape(g_all.shape[0], 1, d))


def _lower_bound(lbp, layer):
    e = jnp.exp(lbp - jnp.max(lbp, axis=0, keepdims=True))
    p = e / jnp.sum(e, axis=0, keepdims=True)
    acc = p[0:1]
    for j in range(1, layer + 1):
        acc = acc + p[j:j + 1]
    return acc - p[0:1]


def _log_forget(f, lb, case):
    ef = jnp.exp(-jnp.abs(f))
    r = 1.0 / (1.0 + ef)
    if case == "none":
        return jnp.log(r) + (jnp.log1p(-lb) + jnp.minimum(f, 0.0))
    sig = jnp.where(f >= 0, 1.0, ef) * r
    arg_pos = jnp.maximum(lb, TINY) + (1.0 - lb) * sig
    if case == "all":
        return jnp.log(arg_pos)
    pos = lb > 0
    arg = jnp.where(pos, arg_pos, r)
    off = jnp.where(pos, 0.0, jnp.log1p(-lb) + jnp.minimum(f, 0.0))
    return jnp.log(arg) + off


def _proj_kernel(*refs, segments, layer, with_cast):
    refs = list(refs)
    x_ref, w_ref = refs[:2]
    pos = 2
    lb_ref = cast_src_ref = cast_dst_ref = None
    if any(mode == "logf" for _, _, mode in segments):
        lb_ref = refs[pos]
        pos += 1
    if with_cast:
        cast_src_ref = refs[pos]
        pos += 1
    o_ref = refs[pos]
    pos += 1
    if with_cast:
        cast_dst_ref = refs[pos]
        pos += 1
    wb_ref = refs[pos]
    j = pl.program_id(0)

    if with_cast:
        cast_dst_ref[...] = cast_src_ref[...].astype(BF16)

    @pl.when(pl.program_id(1) == 0)
    def _():
        wb_ref[...] = w_ref[...].astype(BF16)

    def tile(act):
        tm = x_ref.shape[0]
        nsub = PROJ_SUBTILES * max(tm // PROJ_ROWS, 1)
        sub = tm // nsub if tm % (nsub * BF16_SUBLANES) == 0 else tm
        for r0 in range(0, tm, sub):
            acc = jnp.dot(x_ref[r0:r0 + sub, :], wb_ref[...], preferred_element_type=F32)
            o_ref[r0:r0 + sub, :] = act(acc).astype(o_ref.dtype)

    acts = {
        "id": lambda a: a,
        "silu": _silu,
        "silu_dk": lambda a: _silu(a) * (DK_A ** -0.5),
        "scale_hd": lambda a: a * (HD_B ** -0.5),
    }
    for j0, j1, mode in segments:
        @pl.when((j >= j0) & (j < j1))
        def _(mode=mode):
            if mode != "logf":
                tile(acts[mode])
                return
            lb = _lower_bound(lb_ref[...], layer)
            n_pos = jnp.sum((lb > 0).astype(jnp.int32))
            cases = (("all", n_pos == lb.shape[1]), ("none", n_pos == 0),
                     ("mixed", (n_pos > 0) & (n_pos < lb.shape[1])))
            for case, cond in cases:
                @pl.when(cond)
                def _(case=case):
                    tile(lambda a: _log_forget(a, lb, case))


def proj(xn, w_all, layer, segments, out_dtype, lb_a=None, cast_all=None):
    n, d = xn.shape
    tn = PROJ_COLS
    tm = _row_tile(n, PROJ_ROWS * (4 // jnp.dtype(out_dtype).itemsize))
    bounds, starts, j0 = [], [], 0
    for col0, ncols_seg, mode in segments:
        bounds.append((j0, j0 + ncols_seg // tn, mode))
        starts.append(col0 // tn - j0)
        j0 += ncols_seg // tn
    ncols = j0 * tn

    def w_tile(j):
        t = j + starts[0]
        for (ja, _, _), st in zip(bounds[1:], starts[1:]):
            t = jnp.where(j >= ja, j + st, t)
        return t

    in_specs = [
        pl.BlockSpec((tm, d), lambda j, i: (i, 0)),
        pl.BlockSpec((None, d, tn), lambda j, i: (layer, 0, w_tile(j))),
    ]
    args = [xn, w_all]
    for ja, jz, mode in bounds:
        if mode == "logf":
            in_specs.append(pl.BlockSpec((lb_a.shape[0], tn),
                                         lambda j, i, ja=ja, jz=jz: (0, jnp.clip(j - ja, 0, jz - ja - 1))))
            args.append(lb_a)
    nj, ni = ncols // tn, n // tm
    out_specs = [pl.BlockSpec((tm, tn), lambda j, i: (i, j))]
    out_shape = [jax.ShapeDtypeStruct((n, ncols), out_dtype)]
    if cast_all is not None:
        _, ce, cd = cast_all.shape
        slab = ce // (nj * ni)
        assert slab * nj * ni == ce and slab % BF16_SUBLANES == 0, (ce, nj, ni)
        in_specs.append(pl.BlockSpec((None, slab, cd), lambda j, i: (layer, j * ni + i, 0)))
        args.append(cast_all)
        out_specs.append(pl.BlockSpec((slab, cd), lambda j, i: (j * ni + i, 0)))
        out_shape.append(jax.ShapeDtypeStruct((ce, cd), BF16))
    kern = functools.partial(_proj_kernel, segments=tuple(bounds), layer=layer, with_cast=cast_all is not None)
    res = pl.pallas_call(
        kern,
        grid=(nj, ni),
        in_specs=in_specs,
        out_specs=out_specs,
        out_shape=out_shape,
        scratch_shapes=[pltpu.VMEM((d, tn), BF16)],
        compiler_params=_params("arbitrary", "arbitrary"),
        name="proj",
    )(*args)
    return res if cast_all is not None else res[0]


def _kv_proj_kernel(x_ref, w_ref, o_ref, wb_ref):
    @pl.when(pl.program_id(0) == 0)
    def _():
        wb_ref[...] = w_ref[...].astype(BF16)

    kv = jnp.dot(x_ref[...], wb_ref[...], preferred_element_type=F32)
    ones = jnp.ones((kv.shape[0], HD_B), F32)
    for j in range(KV_B):
        k_j = kv[:, j * HD_B:(j + 1) * HD_B]
        o_ref[j] = jnp.concatenate([k_j, k_j], axis=1)
        v_j = kv[:, (KV_B + j) * HD_B:(KV_B + j + 1) * HD_B]
        o_ref[KV_B + j] = jnp.concatenate([v_j, ones], axis=1)


def kv_proj(xn, w):
    n, d = xn.shape
    tm = _row_tile(n, RES_ROWS)
    return pl.pallas_call(
        _kv_proj_kernel,
        grid=(n // tm,),
        in_specs=[
            pl.BlockSpec((tm, d), lambda i: (i, 0)),
            pl.BlockSpec((d, 2 * KV_B * HD_B), lambda i: (0, 0), pipeline_mode=pl.Buffered(1)),
        ],
        out_specs=pl.BlockSpec((2 * KV_B, tm, 2 * HD_B), lambda i: (0, i, 0)),
        out_shape=jax.ShapeDtypeStruct((2 * KV_B, n, 2 * HD_B), F32),
        scratch_shapes=[pltpu.VMEM((d, 2 * KV_B * HD_B), BF16)],
        compiler_params=_params("arbitrary"),
        name="kv_proj",
    )(xn, w)


def _matmul_res_kernel(y_ref, w_ref, x_ref, g_ref, *out_refs, n_norm, keep_x, head_tiles):
    x = x_ref[...] + jnp.dot(y_ref[...], w_ref[...], preferred_element_type=F32)
    ms = jnp.mean(x * x, axis=-1, keepdims=True)
    xh = x * lax.rsqrt(ms + EPS)
    if head_tiles is not None:
        head_ref, tail_ref = out_refs
        i = pl.program_id(0)

        @pl.when(i < head_tiles)
        def _():
            head_ref[...] = xh * g_ref[0]

        @pl.when(i >= head_tiles)
        def _():
            tail_ref[...] = xh * g_ref[0]
        return
    pos = 0
    if keep_x:
        out_refs[0][...] = x
        pos = 1
    for t in range(n_norm):
        out_refs[pos + t][...] = (xh * g_ref[t]).astype(out_refs[pos + t].dtype)


def matmul_residual(y, w, x, gains, norm_dtype, keep_x=True, split_rows=None):
    n, e = y.shape
    d = w.shape[1]
    gs = jnp.stack([g.reshape(1, d) for g in gains])
    if split_rows is None:
        tm = _row_tile(n, RES_ROWS)
        head_tiles = None
        row_spec = pl.BlockSpec((tm, d), lambda i: (i, 0))
        out_shapes = ([jax.ShapeDtypeStruct((n, d), F32)] if keep_x else []) + \
            [jax.ShapeDtypeStruct((n, d), norm_dtype) for _ in gains]
        out_specs = [row_spec] * len(out_shapes)
    else:
        assert len(gains) == 1 and not keep_x and norm_dtype == F32
        tm = _row_tile(math.gcd(split_rows, n - split_rows), RES_ROWS)
        head_tiles = split_rows // tm
        row_spec = pl.BlockSpec((tm, d), lambda i: (i, 0))
        out_shapes = [jax.ShapeDtypeStruct((split_rows, d), F32), jax.ShapeDtypeStruct((n - split_rows, d), F32)]
        out_specs = [pl.BlockSpec((tm, d), lambda i: (jnp.minimum(i, head_tiles - 1), 0)),
                     pl.BlockSpec((tm, d), lambda i: (jnp.maximum(i - head_tiles, 0), 0))]
    kern = functools.partial(_matmul_res_kernel, n_norm=len(gains), keep_x=keep_x, head_tiles=head_tiles)
    return pl.pallas_call(
        kern,
        grid=(n // tm,),
        in_specs=[
            pl.BlockSpec((tm, e), lambda i: (i, 0)),
            pl.BlockSpec((e, d), lambda i: (0, 0), pipeline_mode=pl.Buffered(1)),
            row_spec,
            pl.BlockSpec((len(gains), 1, d), lambda i: (0, 0, 0)),
        ],
        out_specs=out_specs,
        out_shape=out_shapes,
        compiler_params=_params("arbitrary"),
        name="matmul_residual",
    )(y, w, x, gs)


def _hgrn_kernel(*refs, chunk, n_chunks, hb, per_batch, zero_init, n_alias):
    refs = list(refs)
    zq_ref, zf_ref, zi_ref, zg_ref, on_ref = refs[:5]
    pos = 5
    s0_ref = None
    if not zero_init:
        s0_ref = refs[pos]
        pos += 1
    pos += n_alias
    y_ref, sfin_ref, s_ref, cum_sc, oint_sc, ystage_sc, flag_ref = refs[pos:]
    c = chunk
    t = pl.program_id(2)
    aligned = c % BF16_SUBLANES == 0

    @pl.when(t == 0)
    def _():
        if zero_init:
            s_ref[...] = jnp.zeros_like(s_ref)
        else:
            s_ref[...] = s0_ref[...]

    on_g = on_ref[...]
    if aligned:
        zq_all = zi_all = zg_all = None
    else:
        zq_all = zq_ref[...].astype(F32)
        zi_all = zi_ref[...].astype(F32)
        zg_all = zg_ref[...].astype(F32)

    ri = lax.broadcasted_iota(jnp.int32, (c, c), 0)
    ci_ = lax.broadcasted_iota(jnp.int32, (c, c), 1)
    tril = (ci_ <= ri).astype(BF16)
    row_c = lax.broadcasted_iota(jnp.int32, (c, LANES), 0)
    col_c = lax.broadcasted_iota(jnp.int32, (c, LANES), 1)
    causal = col_c <= row_c
    zpad_k = jnp.zeros((LANES - c, DK_A), F32)
    zpad_x = jnp.zeros((LANES - c - SUBLANES, DK_A), F32)

    def queries(rows, hs):
        return zq_ref[rows, hs].astype(F32) if aligned else zq_all[rows, hs]

    def values(rows, vs_):
        return zi_ref[rows, vs_].astype(F32) if aligned else zi_all[rows, vs_]

    def values_padded(rows, vs_):
        return jnp.concatenate([zi_all[rows, vs_], jnp.zeros((LANES - c, DV_A), F32)], axis=0).astype(BF16)

    def emit_y(o, rows, vs_):
        ms = jnp.mean(o * o, axis=-1, keepdims=True)
        o = o * lax.rsqrt(ms + EPS) * on_g
        if aligned:
            y_ref[rows, vs_] = (o * zg_ref[rows, vs_].astype(F32)).astype(y_ref.dtype)
        else:
            ystage_sc[rows, vs_] = o * zg_all[rows, vs_]

    for ci in range(n_chunks):
        rows = slice(ci * c, (ci + 1) * c)
        si = ci if per_batch else 0
        logf = zf_ref[rows, :]
        h1 = logf.astype(BF16)
        h2 = (logf - h1.astype(F32)).astype(BF16)
        cum = jnp.dot(tril, h1, preferred_element_type=F32) + jnp.dot(tril, h2, preferred_element_type=F32)
        last = cum[c - 1:c, :]
        mid = cum[c // 2 - 1:c // 2, :]
        dev = jnp.max(jnp.maximum(cum[0:1, :] - mid, mid - last))
        flag_ref[ci] = (dev <= FACTOR_SAFE).astype(jnp.int32)
        cum_sc[rows, :] = cum

        for h in range(hb):
            hs = slice(h * DK_A, (h + 1) * DK_A)
            vs_ = slice(h * DV_A, (h + 1) * DV_A)
            q_h = queries(rows, hs)
            k_h = 1.0 - jnp.exp(zf_ref[rows, hs])
            cum_h = cum_sc[rows, hs]
            d_h = cum_h - mid[:, hs]
            qm = (q_h * jnp.exp(d_h)).astype(BF16)
            km = k_h * jnp.exp(-d_h)
            s_prev = s_ref[si, h]
            qa = (q_h * jnp.exp(cum_h)).astype(BF16)
            o_inter = jnp.dot(qa, s_prev.astype(BF16), preferred_element_type=F32)
            oint_sc[rows, vs_] = o_inter
            kd = k_h * jnp.exp(last[:, hs] - cum_h)
            el_rows = jnp.broadcast_to(jnp.exp(last[:, hs]), (SUBLANES, DK_A))
            x_t = jnp.concatenate([kd, el_rows, zpad_x], axis=0).T
            if aligned:
                att = lax.dot_general(qm, km.astype(BF16), (((1,), (1,)), ((), ())), preferred_element_type=F32)
                att = jnp.where(causal[:, :c], att, 0.0).astype(BF16)
                lhs = jnp.concatenate([att, x_t[:, :c].astype(BF16)], axis=0)
                res = jnp.dot(lhs, zi_ref[rows, vs_], preferred_element_type=F32)
                o_intra, ds = res[:c], res[c:]
            else:
                v_pad = values_padded(rows, vs_)
                km_pad = jnp.concatenate([km, zpad_k], axis=0).astype(BF16)
                att = lax.dot_general(qm, km_pad, (((1,), (1,)), ((), ())), preferred_element_type=F32)
                att = jnp.where(causal, att, 0.0).astype(BF16)
                o_intra = jnp.dot(att, v_pad, preferred_element_type=F32)
                ds = jnp.dot(x_t.astype(BF16), v_pad, preferred_element_type=F32)
            s_ref[si, h] = x_t[:, c:c + 1] * s_prev + ds
            emit_y(o_inter + o_intra, rows, vs_)

    for ci in range(n_chunks):
        rows = slice(ci * c, (ci + 1) * c)

        @pl.when(flag_ref[ci] == 0)
        def _(rows=rows):
            row_v = lax.broadcasted_iota(jnp.int32, (c, 1), 0)

            def row_of(a, s):
                return jnp.sum(jnp.where(row_v == s, a, 0.0), axis=0, keepdims=True)

            for h in range(hb):
                hs = slice(h * DK_A, (h + 1) * DK_A)
                vs_ = slice(h * DV_A, (h + 1) * DV_A)
                q_h = queries(rows, hs)
                k_h = 1.0 - jnp.exp(zf_ref[rows, hs])
                cum_h = cum_sc[rows, hs]
                v_h = values(rows, vs_)

                def body(s, acc_o, q_h=q_h, k_h=k_h, cum_h=cum_h, v_h=v_h):
                    dec = jnp.exp(jnp.minimum(cum_h - row_of(cum_h, s), 0.0))
                    w = jnp.sum(q_h * row_of(k_h, s) * dec, axis=-1, keepdims=True)
                    w = jnp.where(row_v >= s, w, 0.0)
                    return acc_o + w * row_of(v_h, s)

                o_intra = lax.fori_loop(0, c, body, jnp.zeros((c, DV_A), F32))
                emit_y(oint_sc[rows, vs_] + o_intra, rows, vs_)

    if not aligned:
        y_ref[...] = ystage_sc[...].astype(y_ref.dtype)

    @pl.when(t == pl.num_programs(2) - 1)
    def _():
        sfin_ref[...] = s_ref[...]


def hgrn_scan(zf, zigq, onorm_a, s0_all, y_prev, states_prev, *, layer, n_batch, rows_per_batch, row0, chunk,
              n_chunks, hb, per_batch):
    n = zf.shape[0]
    tb = chunk * n_chunks
    ns = n_chunks if per_batch else 1
    nt = 1 if per_batch else rows_per_batch // tb
    nbg = n_batch // ns
    ng = H_A // hb
    wk, wv = hb * DK_A, hb * DV_A
    q_off = (2 * E_WIDTH) // wk
    g_off = E_WIDTH // wv
    if per_batch:
        rb0 = row0 // tb
        row = lambda b, g, t: rb0 + b
    else:
        assert tb == ROW_BLOCK and row0 == 0
        row = lambda b, g, t: _prompt_block(b, t, n_batch, nt)
    in_specs = [
        pl.BlockSpec((tb, wk), lambda b, g, t: (row(b, g, t), q_off + g)),
        pl.BlockSpec((tb, wk), lambda b, g, t: (row(b, g, t), g)),
        pl.BlockSpec((tb, wv), lambda b, g, t: (row(b, g, t), g)),
        pl.BlockSpec((tb, wv), lambda b, g, t: (row(b, g, t), g_off + g)),
        pl.BlockSpec((None, 1, DV_A), lambda b, g, t: (layer, 0, 0)),
    ]
    args = [zigq, zf, zigq, zigq, onorm_a.reshape(N_A, 1, DV_A)]
    state_spec = pl.BlockSpec((None, ns, hb, DK_A, DV_A), lambda b, g, t: (layer, b, g, 0, 0))
    if s0_all is not None:
        in_specs.append(state_spec)
        args.append(s0_all)
    aliases = {}
    for out_idx, prev in ((0, y_prev), (1, states_prev)):
        if prev is not None:
            aliases[len(args)] = out_idx
            in_specs.append(pl.BlockSpec(memory_space=pl.ANY))
            args.append(prev)
    kern = functools.partial(_hgrn_kernel, chunk=chunk, n_chunks=n_chunks, hb=hb, per_batch=per_batch,
                             zero_init=s0_all is None, n_alias=len(aliases))
    return pl.pallas_call(
        kern,
        grid=(nbg, ng, nt),
        in_specs=in_specs,
        out_specs=[
            pl.BlockSpec((tb, wv), lambda b, g, t: (row(b, g, t), g)),
            state_spec,
        ],
        out_shape=[
            jax.ShapeDtypeStruct((n, E_WIDTH), BF16),
            jax.ShapeDtypeStruct((N_A, n_batch, H_A, DK_A, DV_A), F32),
        ],
        scratch_shapes=[
            pltpu.VMEM((ns, hb, DK_A, DV_A), F32),
            pltpu.VMEM((tb, wk), F32),
            pltpu.VMEM((tb, wv), F32),
            pltpu.VMEM((tb, wv), F32),
            pltpu.SMEM((n_chunks,), jnp.int32),
        ],
        input_output_aliases=aliases,
        compiler_params=_params("arbitrary", "arbitrary", "arbitrary"),
        name=f"hgrn_scan_c{chunk}",
    )(*args)


def _swa_prompt_kernel(sink_ref, q_ref, g_ref, kc_ref, kp_ref, vc_ref, vp_ref, y_ref, d_sc, s_sc, p_sc):
    blk = pl.program_id(1)
    kg = pl.program_id(2)
    w = WINDOW
    npair = G_B // 2

    @pl.when(kg == 0)
    def _():
        row = lax.broadcasted_iota(jnp.int32, (w, 2 * w), 0)
        col = lax.broadcasted_iota(jnp.int32, (w, 2 * w), 1)
        dist = w + row - col
        key_pos = (blk - 1) * w + col
        valid = (dist >= 0) & (dist < w) & (key_pos >= PAD)
        d_sc[...] = jnp.where(valid, dist.astype(F32), -NEG_BIG)

    lane = lax.broadcasted_iota(jnp.int32, (w, LANES), 1)
    lane_k = lax.broadcasted_iota(jnp.int32, (2 * w, LANES), 1)
    row_k = lax.broadcasted_iota(jnp.int32, (2 * w, LANES), 0)
    for kk in range(KH_STEP):
        kh = kg * KH_STEP + kk
        kdup = jnp.concatenate([kp_ref[kk], kc_ref[kk]], axis=0)
        k_par = [jnp.where(lane_k < HD_B, kdup, 0.0).astype(BF16), jnp.where(lane_k < HD_B, 0.0, kdup).astype(BF16)]
        vraw = jnp.concatenate([vp_ref[kk], vc_ref[kk]], axis=0)
        vext = jnp.where(row_k == 0, jnp.where(lane_k < HD_B, 0.0, vraw), vraw).astype(BF16)
        qs = jnp.concatenate(
            [q_ref[:, (kk * npair + j) * LANES:(kk * npair + j + 1) * LANES] for j in range(npair)],
            axis=0).astype(BF16)
        for par in range(2):
            s_sc[kk, par] = lax.dot_general(qs, k_par[par], (((1,), (1,)), ((), ())),
                                            preferred_element_type=F32)
        for j in range(npair):
            rows = slice(j * w, (j + 1) * w)
            for par in range(2):
                head = kh * G_B + 2 * j + par
                slope = jnp.exp(-LN2 * 8.0 * (head + 1).astype(F32) / H_B)
                sink = sink_ref[head]
                s_lo = s_sc[kk, par, rows, :w] - slope * d_sc[:, :w]
                s_lo = jnp.where(lane == 0, sink, s_lo)
                s_hi = s_sc[kk, par, rows, w:] - slope * d_sc[:, w:]
                m = jnp.max(jnp.maximum(s_lo, s_hi), axis=-1, keepdims=True)
                p_sc[kk, par, rows, :w] = jnp.exp(s_lo - m).astype(BF16)
                p_sc[kk, par, rows, w:] = jnp.exp(s_hi - m).astype(BF16)
        o_par = [jnp.dot(p_sc[kk, par], vext, preferred_element_type=F32) for par in range(2)]
        for j in range(npair):
            o_even = o_par[0][j * w:(j + 1) * w, :]
            o_odd = o_par[1][j * w:(j + 1) * w, :]
            num = jnp.where(lane < HD_B, o_even, pltpu.roll(o_odd, HD_B, 1))
            den = jnp.where(lane < HD_B, pltpu.roll(o_even, HD_B, 1), o_odd)
            cs = slice((kk * npair + j) * LANES, (kk * npair + j + 1) * LANES)
            y_ref[:, cs] = (num / den * g_ref[:, cs].astype(F32)).astype(BF16)


def swa_prompt(zb, kv_hm, sink, n_batch, blocks_per_batch):
    n = zb.shape[0]
    w = WINDOW
    gw = KH_STEP * G_B * HD_B
    nb = blocks_per_batch
    nkg = KV_B // KH_STEP
    cur = lambda b, i, kg: _prompt_block(b, i, n_batch, nb)
    prev = lambda b, i, kg: _prompt_block(b, jnp.maximum(i - 1, 0), n_batch, nb)
    kv_block = (KH_STEP, w, 2 * HD_B)
    return pl.pallas_call(
        _swa_prompt_kernel,
        grid=(n_batch, nb, nkg),
        in_specs=[
            pl.BlockSpec(memory_space=pltpu.SMEM),
            pl.BlockSpec((w, gw), lambda b, i, kg: (cur(b, i, kg), kg)),
            pl.BlockSpec((w, gw), lambda b, i, kg: (cur(b, i, kg), nkg + kg)),
            pl.BlockSpec(kv_block, lambda b, i, kg: (kg, cur(b, i, kg), 0)),
            pl.BlockSpec(kv_block, lambda b, i, kg: (kg, prev(b, i, kg), 0)),
            pl.BlockSpec(kv_block, lambda b, i, kg: (nkg + kg, cur(b, i, kg), 0)),
            pl.BlockSpec(kv_block, lambda b, i, kg: (nkg + kg, prev(b, i, kg), 0)),
        ],
        out_specs=pl.BlockSpec((w, gw), lambda b, i, kg: (cur(b, i, kg), kg)),
        out_shape=jax.ShapeDtypeStruct((n, E_WIDTH), BF16),
        scratch_shapes=[
            pltpu.VMEM((w, 2 * w), F32),
            pltpu.VMEM((KH_STEP, 2, G_B // 2 * w, 2 * w), F32),
            pltpu.VMEM((KH_STEP, 2, G_B // 2 * w, 2 * w), BF16),
        ],
        compiler_params=_params("arbitrary", "arbitrary", "arbitrary"),
        name="swa_prompt",
    )(sink, zb, zb, kv_hm, kv_hm, kv_hm, kv_hm)


def _swa_sample_kernel(sink_ref, q_ref, g_ref, kn_ref, vn_ref, ck_ref, cv_ref, yprev_ref, y_ref, *, t_new, nbs):
    del yprev_ref
    wb = ck_ref.shape[1]
    nk = wb + t_new
    nr = G_B * t_new
    row = lax.broadcasted_iota(jnp.int32, (nr, nk), 0)
    col = lax.broadcasted_iota(jnp.int32, (nr, nk), 1)
    tok = row % t_new
    grp = row // t_new
    dist = wb + tok - col
    valid = (dist >= 0) & (dist < WINDOW)
    distf = dist.astype(F32)
    grp1 = lax.broadcasted_iota(jnp.int32, (nr, 1), 0) // t_new
    q_all = q_ref[...].astype(F32)
    g_all = g_ref[...].astype(F32)
    for kh in range(KV_B):
        ks = slice(kh * HD_B, (kh + 1) * HD_B)
        slope = jnp.exp(-LN2 * 8.0 * (kh * G_B + grp + 1).astype(F32) / H_B)
        sink = jnp.zeros((nr, 1), F32)
        for g in range(G_B):
            sink = jnp.where(grp1 == g, sink_ref[kh * G_B + g], sink)
        outs = []
        for bi in range(nbs):
            rb = slice(bi * t_new, (bi + 1) * t_new)
            kf = jnp.concatenate([ck_ref[bi, :, ks], kn_ref[kh][rb, :HD_B]], axis=0).astype(BF16)
            vf = jnp.concatenate([cv_ref[bi, :, ks], vn_ref[kh][rb, :HD_B]], axis=0).astype(BF16)
            qs = jnp.concatenate(
                [q_all[rb, (kh * G_B + g) * HD_B:(kh * G_B + g + 1) * HD_B] for g in range(G_B)], axis=0)
            s = lax.dot_general(qs.astype(BF16), kf, (((1,), (1,)), ((), ())), preferred_element_type=F32)
            s = s - slope * distf
            s = jnp.where(valid, s, NEG_BIG)
            m = jnp.maximum(jnp.max(s, axis=-1, keepdims=True), sink)
            p = jnp.where(valid, jnp.exp(s - m), 0.0)
            den = jnp.sum(p, axis=-1, keepdims=True) + jnp.exp(sink - m)
            outs.append(jnp.dot(p.astype(BF16), vf, preferred_element_type=F32) / den)
        for g in range(G_B):
            hsl = slice((kh * G_B + g) * HD_B, (kh * G_B + g + 1) * HD_B)
            o = jnp.concatenate([outs[bi][g * t_new:(g + 1) * t_new, :] for bi in range(nbs)], axis=0)
            y_ref[:, hsl] = (o * g_all[:, hsl]).astype(y_ref.dtype)


def swa_sample(zb, kv_hm, sink, y_prev, cache_k, cache_v, *, row0, n_batch, t_new, nbs):
    n = zb.shape[0]
    wb = cache_k.shape[1]
    tb = nbs * t_new
    rb0 = row0 // tb
    kern = functools.partial(_swa_sample_kernel, t_new=t_new, nbs=nbs)
    return pl.pallas_call(
        kern,
        grid=(n_batch // nbs,),
        in_specs=[
            pl.BlockSpec(memory_space=pltpu.SMEM),
            pl.BlockSpec((tb, E_WIDTH), lambda b: (rb0 + b, 0)),
            pl.BlockSpec((tb, E_WIDTH), lambda b: (rb0 + b, 1)),
            pl.BlockSpec((KV_B, tb, 2 * HD_B), lambda b: (0, rb0 + b, 0)),
            pl.BlockSpec((KV_B, tb, 2 * HD_B), lambda b: (1, rb0 + b, 0)),
            pl.BlockSpec((nbs, wb, KV_B * HD_B), lambda b: (b, 0, 0)),
            pl.BlockSpec((nbs, wb, KV_B * HD_B), lambda b: (b, 0, 0)),
            pl.BlockSpec(memory_space=pl.ANY),
        ],
        out_specs=pl.BlockSpec((tb, E_WIDTH), lambda b: (rb0 + b, 0)),
        out_shape=jax.ShapeDtypeStruct((n, E_WIDTH), BF16),
        input_output_aliases={7: 0},
        compiler_params=_params("arbitrary"),
        name="swa_sample",
    )(sink, zb, zb, kv_hm, kv_hm, cache_k, cache_v, y_prev)


def kernel(x_prompt, x_sample, state_hgrn, cache_k, cache_v, meta_tokens, norm_a, w_in_a, lb_a, onorm_a,
           w_out_a, norm_kv, w_kv, norm_b, w_in_b, sink_b, w_out_b, norm_f):
    bsz, seq, d = x_prompt.shape
    dec_b, dec_t, _ = x_sample.shape
    wb = cache_k.shape[1]

    rows_p = PAD + N_META + seq
    nb_p = rows_p // ROW_BLOCK
    n_p = bsz * rows_p
    n_s = dec_b * dec_t
    fd = 2 * H_A * DK_A
    nbs = max(1, BF16_SUBLANES // dec_t)
    s0_s = state_hgrn.astype(F32)
    ck = cache_k.reshape(dec_b, wb, KV_B * HD_B)
    cv = cache_v.reshape(dec_b, wb, KV_B * HD_B)

    x, xn = assemble(x_prompt, meta_tokens, x_sample.reshape(n_s, d), norm_a, 0)
    sp = ss = None
    for layer in range(N_A):
        zf, w_out16 = proj(xn, w_in_a, layer, ((fd // 2, fd // 2, "logf"),), F32, lb_a=lb_a, cast_all=w_out_a)
        zigq = proj(xn, w_in_a, layer, ((fd, E_WIDTH, "id"), (fd + E_WIDTH, E_WIDTH, "silu"),
                                        (0, fd // 2, "silu_dk")), BF16)
        y, sp = hgrn_scan(zf, zigq, onorm_a, None, None, sp, layer=layer, n_batch=bsz, rows_per_batch=rows_p,
                          row0=0, chunk=HGRN_CHUNK, n_chunks=ROW_BLOCK // HGRN_CHUNK, hb=HGRN_HEADS_PROMPT,
                          per_batch=False)
        y, ss = hgrn_scan(zf, zigq, onorm_a, s0_s, y, ss, layer=layer, n_batch=dec_b, rows_per_batch=dec_t,
                          row0=n_p, chunk=dec_t, n_chunks=nbs, hb=HGRN_HEADS_SAMPLE, per_batch=True)
        if layer + 1 < N_A:
            x, xn = matmul_residual(y, w_out16, x, (norm_a[layer + 1],), BF16)
        else:
            x, xn, xn_kv = matmul_residual(y, w_out16, x, (norm_b[0], norm_kv), BF16)
    kv_hm = kv_proj(xn_kv, w_kv)
    for layer in range(N_B):
        zb, w_out16 = proj(xn, w_in_b, layer, ((0, E_WIDTH, "scale_hd"), (E_WIDTH, E_WIDTH, "silu")), BF16,
                           cast_all=w_out_b)
        y = swa_prompt(zb, kv_hm, sink_b[layer], bsz, nb_p)
        y = swa_sample(zb, kv_hm, sink_b[layer], y, ck, cv, row0=n_p, n_batch=dec_b, t_new=dec_t, nbs=nbs)
        if layer + 1 < N_B:
            x, xn = matmul_residual(y, w_out16, x, (norm_b[layer + 1],), BF16)
        else:
            out_p, out_t = matmul_residual(y, w_out16, x, (norm_f,), F32, keep_x=False, split_rows=bsz * seq)

    y_prompt = out_p.reshape(bsz, seq, d)
    y_sample = out_t[n_p - bsz * seq:].reshape(dec_b, dec_t, d)
    kvp = jnp.stack([kv_hm[:, (b + 1) * seq - wb:(b + 1) * seq, :HD_B] for b in range(bsz)], axis=1)
    kvp = kvp.reshape(2, KV_B, bsz, wb, HD_B)
    cache_k_prompt = jnp.transpose(kvp[0], (1, 2, 0, 3)).astype(cache_k.dtype)
    cache_v_prompt = jnp.transpose(kvp[1], (1, 2, 0, 3)).astype(cache_v.dtype)
    kvs = kv_hm[:, n_p:, :HD_B].reshape(2, KV_B, dec_b, dec_t, HD_B)
    k_new = jnp.transpose(kvs[0], (1, 2, 0, 3)).astype(cache_k.dtype)
    v_new = jnp.transpose(kvs[1], (1, 2, 0, 3)).astype(cache_v.dtype)
    cache_k_sample = jnp.concatenate([cache_k, k_new], axis=1)[:, -wb:]
    cache_v_sample = jnp.concatenate([cache_v, v_new], axis=1)[:, -wb:]

    return (y_prompt, y_sample, sp.astype(state_hgrn.dtype), cache_k_prompt, cache_v_prompt,
            ss.astype(state_hgrn.dtype), cache_k_sample, cache_v_sample)
```
